```python
import jax, jax.numpy as jnp
from jax import lax
import numpy as np

D_MODEL = 2048
BATCH = 8
SEQ = 4096
DEPTH = 4

MEM_TOKENS = 256
EPS = 1e-6
ROPE_THETA = 10000.0
Q_BLOCK = 128
MLA_HEADS = D_MODEL // 256
QK_NOPE_DIM = 128
QK_ROPE_DIM = 64
QK_HEAD_DIM = QK_NOPE_DIM + QK_ROPE_DIM
V_HEAD_DIM = 128
Q_LORA_RANK = D_MODEL // 4
KV_LORA_RANK = D_MODEL // 8
MLA_WIDTH = MLA_HEADS * V_HEAD_DIM
CONV_WIDTH = D_MODEL // 4
CONV_K = 3
MEM_HEADS = 4
MEM_HEAD_DIM = D_MODEL // 16
MEM_WIDTH = MEM_HEADS * MEM_HEAD_DIM
MIX_WIDTH = MLA_WIDTH + CONV_WIDTH + MEM_WIDTH
IN_SPLITS = (Q_LORA_RANK, KV_LORA_RANK, QK_ROPE_DIM, CONV_WIDTH, CONV_WIDTH, CONV_WIDTH, MEM_WIDTH, MIX_WIDTH)
IN_COLS = Q_LORA_RANK + KV_LORA_RANK + QK_ROPE_DIM + 3 * CONV_WIDTH + MEM_WIDTH + MIX_WIDTH

kernel_name = "hybrid_mla_shortconv_memory_encoder"


def rmsnorm(x, g):
    x32 = x.astype(jnp.float32)
    y = x32 * lax.rsqrt(jnp.mean(x32 * x32, axis=-1, keepdims=True) + EPS)
    return (y * g.astype(jnp.float32)).astype(x.dtype)


def rope_tables(positions):
    inv_freq = 1.0 / (ROPE_THETA ** (jnp.arange(0, QK_ROPE_DIM, 2, dtype=jnp.float32) / QK_ROPE_DIM))
    ang = positions.astype(jnp.float32)[..., None] * inv_freq
    return jnp.cos(ang), jnp.sin(ang)


def apply_rope(x, cos, sin):
    half = x.shape[-1] // 2
    x32 = x.astype(jnp.float32)
    x1, x2 = x32[..., :half], x32[..., half:]
    return jnp.concatenate([x1 * cos - x2 * sin, x2 * cos + x1 * sin], axis=-1).astype(x.dtype)


def split_cols(z):
    idx = list(np.cumsum(IN_SPLITS)[:-1])
    return jnp.split(z, idx, axis=-1)


def mla_attention(q, k, v):
    b, s, h, dq = q.shape
    nb = s // Q_BLOCK
    scale = QK_HEAD_DIM ** -0.5
    qb = q.reshape(b, nb, Q_BLOCK, h, dq).transpose(1, 0, 2, 3, 4)

    def block(qi):
        sc = jnp.einsum('bqhd,bkhd->bhqk', qi, k).astype(jnp.float32) * scale
        p = jax.nn.softmax(sc, axis=-1).astype(v.dtype)
        return jnp.einsum('bhqk,bkhd->bqhd', p, v)

    o = lax.map(block, qb)
    return o.transpose(1, 0, 2, 3, 4).reshape(b, s, h * V_HEAD_DIM)


def short_gated_conv(gb, gc, xin, w):
    u = gc * xin
    up = jnp.pad(u, ((0, 0), (1, 1), (0, 0)))
    conv = up[:, :-2] * w[0] + up[:, 1:-1] * w[1] + up[:, 2:] * w[2]
    return gb * conv


def memory_attention(q, mem_n, w_mk, w_mv):
    b, m, _ = mem_n.shape
    mk = (mem_n @ w_mk).reshape(b, m, MEM_HEADS, MEM_HEAD_DIM)
    mv = (mem_n @ w_mv).reshape(b, m, MEM_HEADS, MEM_HEAD_DIM)
    sc = jnp.einsum('bshd,bmhd->bhsm', q, mk).astype(jnp.float32) * (MEM_HEAD_DIM ** -0.5)
    p = jax.nn.softmax(sc, axis=-1).astype(mv.dtype)
    o = jnp.einsum('bhsm,bmhd->bshd', p, mv)
    return o.reshape(b, q.shape[1], MEM_WIDTH)


def setup_inputs(seed: int = 0) -> dict:
    key = jax.random.key(seed)
    ks = jax.random.split(key, 16)
    f32 = jnp.float32

    def w(k, shape, fan_in):
        return jax.random.normal(k, shape, f32) * (fan_in ** -0.5)

    def gain(k, shape):
        return 1.0 + 0.02 * jax.random.normal(k, shape, f32)

    x = jax.random.normal(ks[0], (BATCH, SEQ, D_MODEL), f32)
    mem = jax.random.normal(ks[1], (BATCH, MEM_TOKENS, D_MODEL), f32)
    offset = jax.random.randint(ks[2], (BATCH, 1), 0, 4096, dtype=jnp.int32)
    positions = (offset + jnp.arange(SEQ, dtype=jnp.int32)[None, :]).astype(jnp.int32)
    return {
        "x": x,
        "mem": mem,
        "positions": positions,
        "pre_norm_g": gain(ks[3], (DEPTH, D_MODEL)),
        "w_in": w(ks[4], (DEPTH, D_MODEL, IN_COLS), D_MODEL),
        "q_norm_g": gain(ks[5], (DEPTH, Q_LORA_RANK)),
        "w_uq": w(ks[6], (DEPTH, Q_LORA_RANK, MLA_HEADS * QK_HEAD_DIM), Q_LORA_RANK),
        "kv_norm_g": gain(ks[7], (DEPTH, KV_LORA_RANK)),
        "w_ukv": w(ks[8], (DEPTH, KV_LORA_RANK, MLA_HEADS * (QK_NOPE_DIM + V_HEAD_DIM)), KV_LORA_RANK),
        "conv_w": w(ks[9], (DEPTH, CONV_K, CONV_WIDTH), CONV_K),
        "mem_norm_g": gain(ks[10], (DEPTH, D_MODEL)),
        "w_mk": w(ks[11], (DEPTH, D_MODEL, MEM_WIDTH), D_MODEL),
        "w_mv": w(ks[12], (DEPTH, D_MODEL, MEM_WIDTH), D_MODEL),
        "w_o": w(ks[13], (DEPTH, MIX_WIDTH, D_MODEL), MIX_WIDTH),
        "post_norm_g": gain(ks[14], (DEPTH, D_MODEL)),
    }


def reference(x, mem, positions, pre_norm_g, w_in, q_norm_g, w_uq, kv_norm_g, w_ukv, conv_w,
              mem_norm_g, w_mk, w_mv, w_o, post_norm_g):
    b, s, _ = x.shape
    cos, sin = rope_tables(positions)
    for l in range(DEPTH):
        h = rmsnorm(x, pre_norm_g[l])
        z = h @ w_in[l]
        q_lat, kv_lat, k_pe, gb, gc, xin, q_mem, gate = split_cols(z)

        q = (rmsnorm(q_lat, q_norm_g[l]) @ w_uq[l]).reshape(b, s, MLA_HEADS, QK_HEAD_DIM)
        q = jnp.concatenate([q[..., :QK_NOPE_DIM],
                             apply_rope(q[..., QK_NOPE_DIM:], cos[:, :, None, :], sin[:, :, None, :])], axis=-1)
        kv = (rmsnorm(kv_lat, kv_norm_g[l]) @ w_ukv[l]).reshape(b, s, MLA_HEADS, QK_NOPE_DIM + V_HEAD_DIM)
        k_nope, v = kv[..., :QK_NOPE_DIM], kv[..., QK_NOPE_DIM:]
        k_pe = apply_rope(k_pe, cos, sin)
        k = jnp.concatenate([k_nope, jnp.broadcast_to(k_pe[:, :, None, :], (b, s, MLA_HEADS, QK_ROPE_DIM))], axis=-1)
        a_out = mla_attention(q, k, v)

        c_out = short_gated_conv(gb, gc, xin, conv_w[l])

        mem_n = rmsnorm(mem, mem_norm_g[l])
        m_out = memory_attention(q_mem.reshape(b, s, MEM_HEADS, MEM_HEAD_DIM), mem_n, w_mk[l], w_mv[l])

        y = jnp.concatenate([a_out, c_out, m_out], axis=-1) * jax.nn.silu(gate)
        o = y @ w_o[l]
        x = x + rmsnorm(o, post_norm_g[l])
    return x
```

```python
import functools
import math

import jax
import jax.numpy as jnp
from jax import lax
from jax.experimental import pallas as pl
from jax.experimental.pallas import tpu as pltpu

D_MODEL = 2048
DEPTH = 4
MEM_TOKENS = 256
EPS = 1e-6
ROPE_THETA = 10000.0
MLA_HEADS = 8
QK_NOPE_DIM = 128
QK_ROPE_DIM = 64
QK_HEAD_DIM = QK_NOPE_DIM + QK_ROPE_DIM
V_HEAD_DIM = 128
Q_LORA_RANK = 512
KV_LORA_RANK = 256
MLA_WIDTH = MLA_HEADS * V_HEAD_DIM
CONV_WIDTH = 512
MEM_HEADS = 4
MEM_HEAD_DIM = 128
MEM_WIDTH = MEM_HEADS * MEM_HEAD_DIM
MIX_WIDTH = MLA_WIDTH + CONV_WIDTH + MEM_WIDTH

LANES = 128
SUBLANES = 8
QK_PAD_DIM = QK_NOPE_DIM + LANES

BF16 = jnp.bfloat16
F32 = jnp.float32

PREP_ROWS = 512
MIX_ROWS = 512
OUT_ROWS = 512
ATTN_Q = 256
ATTN_K = 512
VMEM_LIMIT = 56 * 1024 * 1024

_NT = (((1,), (1,)), ((), ()))


def _rms(x, g):
    return x * lax.rsqrt(jnp.mean(x * x, axis=-1, keepdims=True) + EPS) * g


def _dot(a, b):
    return jnp.dot(a, b, preferred_element_type=F32)


def _dot_nt(a, b):
    return lax.dot_general(a, b, _NT, preferred_element_type=F32)


def _mla_prep_kernel(x_ref, pos_ref, invf_ref, gpre_ref, w_in_ref, gq_ref, w_uq_ref, gkv_ref, w_k_ref,
                     w_vt_ref, q_ref, k_ref, vt_ref):
    h = _rms(x_ref[...], gpre_ref[...]).astype(BF16)
    z = _dot(h, w_in_ref[...])
    qn = _rms(z[:, :Q_LORA_RANK], gq_ref[...]).astype(BF16)
    kvn = _rms(z[:, Q_LORA_RANK:Q_LORA_RANK + KV_LORA_RANK], gkv_ref[...]).astype(BF16)
    kpe = z[:, Q_LORA_RANK + KV_LORA_RANK:]

    ang = pos_ref[...].astype(F32) * invf_ref[...]
    cos = jnp.cos(ang)
    sin = jnp.sin(ang)
    lane = lax.broadcasted_iota(jnp.int32, ang.shape, 1)
    half = QK_ROPE_DIM // 2
    sin_lo = jnp.where(lane < half, -sin, 0.0)
    sin_hi = jnp.where((lane >= half) & (lane < QK_ROPE_DIM), sin, 0.0)

    def rope(t):
        return t * cos + pltpu.roll(t, half, 1) * sin_hi + pltpu.roll(t, LANES - half, 1) * sin_lo

    q_scale = (QK_HEAD_DIM ** -0.5) * math.log2(math.e)
    q = _dot(qn, w_uq_ref[...])
    kn = _dot(kvn, w_k_ref[...])
    vt = _dot_nt(w_vt_ref[...], kvn)
    kpe_r = rope(kpe).astype(BF16)
    for hd in range(MLA_HEADS):
        lo, hi = hd * LANES, (hd + 1) * LANES
        q_ref[0, hd, :, :QK_NOPE_DIM] = (q[:, lo:hi] * q_scale).astype(BF16)
        q_ref[0, hd, :, QK_NOPE_DIM:] = (rope(q[:, MLA_WIDTH + lo:MLA_WIDTH + hi]) * q_scale).astype(BF16)
        k_ref[0, hd, :, :QK_NOPE_DIM] = kn[:, lo:hi].astype(BF16)
        k_ref[0, hd, :, QK_NOPE_DIM:] = kpe_r
        vt_ref[0, hd] = vt[lo:hi, :].astype(BF16)


def _mla_prep(x2, pos2, invf, gpre, w_in_mla, gq, w_uq_p, gkv, w_k, w_vt, batch, seq):
    rows = PREP_ROWS
    per_seq = seq // rows
    const = lambda i: (0, 0)
    head_rows = lambda i: (i // per_seq, 0, i % per_seq, 0)
    return pl.pallas_call(
        _mla_prep_kernel,
        grid=(batch * seq // rows,),
        in_specs=[
            pl.BlockSpec((rows, D_MODEL), lambda i: (i, 0)),
            pl.BlockSpec((rows, 1), lambda i: (i, 0)),
            pl.BlockSpec(invf.shape, const),
            pl.BlockSpec(gpre.shape, const),
            pl.BlockSpec(w_in_mla.shape, const),
            pl.BlockSpec(gq.shape, const),
            pl.BlockSpec(w_uq_p.shape, const),
            pl.BlockSpec(gkv.shape, const),
            pl.BlockSpec(w_k.shape, const),
            pl.BlockSpec(w_vt.shape, const),
        ],
        out_specs=[
            pl.BlockSpec((1, MLA_HEADS, rows, QK_PAD_DIM), head_rows),
            pl.BlockSpec((1, MLA_HEADS, rows, QK_PAD_DIM), head_rows),
            pl.BlockSpec((1, MLA_HEADS, V_HEAD_DIM, rows), lambda i: (i // per_seq, 0, 0, i % per_seq)),
        ],
        out_shape=[
            jax.ShapeDtypeStruct((batch, MLA_HEADS, seq, QK_PAD_DIM), BF16),
            jax.ShapeDtypeStruct((batch, MLA_HEADS, seq, QK_PAD_DIM), BF16),
            jax.ShapeDtypeStruct((batch, MLA_HEADS, V_HEAD_DIM, seq), BF16),
        ],
        compiler_params=pltpu.CompilerParams(dimension_semantics=("parallel",), vmem_limit_bytes=VMEM_LIMIT),
        name="mla_prep",
    )(x2, pos2, invf, gpre, w_in_mla, gq, w_uq_p, gkv, w_k, w_vt)


def _mla_attn_kernel(q_ref, k_ref, vt_ref, o_ref, *, seq):
    q = q_ref[0, 0]

    def chunk(c, carry):
        m, l, acc = carry
        start = pl.multiple_of(c * ATTN_K, ATTN_K)
        s = _dot_nt(k_ref[0, 0, pl.ds(start, ATTN_K), :], q)
        m_new = jnp.maximum(m, jnp.max(s, axis=0, keepdims=True))
        alpha = jnp.exp2(m - m_new)
        p = jnp.exp2(s - m_new)
        l = l * alpha + jnp.sum(p, axis=0, keepdims=True)
        acc = acc * alpha + _dot(vt_ref[0, 0, :, pl.ds(start, ATTN_K)], p.astype(BF16))
        return m_new, l, acc

    init = (jnp.full((1, ATTN_Q), -jnp.inf, F32), jnp.zeros((1, ATTN_Q), F32), jnp.zeros((V_HEAD_DIM, ATTN_Q), F32))
    _, l, acc = lax.fori_loop(0, seq // ATTN_K, chunk, init)
    o_ref[0] = (acc * (1.0 / l)).T.astype(o_ref.dtype)


def _mla_attn(q, k, vt):
    batch, heads, seq, _ = q.shape
    return pl.pallas_call(
        functools.partial(_mla_attn_kernel, seq=seq),
        grid=(batch, heads, seq // ATTN_Q),
        in_specs=[
            pl.BlockSpec((1, 1, ATTN_Q, QK_PAD_DIM), lambda b, h, i: (b, h, i, 0)),
            pl.BlockSpec((1, 1, seq, QK_PAD_DIM), lambda b, h, i: (b, h, 0, 0)),
            pl.BlockSpec((1, 1, V_HEAD_DIM, seq), lambda b, h, i: (b, h, 0, 0)),
        ],
        out_specs=pl.BlockSpec((1, ATTN_Q, V_HEAD_DIM), lambda b, h, i: (b, i, h)),
        out_shape=jax.ShapeDtypeStruct((batch, seq, MLA_WIDTH), BF16),
        compiler_params=pltpu.CompilerParams(
            dimension_semantics=("parallel", "parallel", "parallel"), vmem_limit_bytes=VMEM_LIMIT),
        name="mla_attn",
    )(q, k, vt)


def _mem_kv_kernel(mem_ref, g_ref, w_mk_ref, w_mv_ref, mk_ref, mv_ref):
    mem_n = _rms(mem_ref[0], g_ref[0]).astype(BF16)
    mk_ref[0, 0] = _dot(mem_n, w_mk_ref[0]).astype(BF16)
    mv_ref[0, 0] = _dot(mem_n, w_mv_ref[0]).astype(BF16)


def _mem_kv(mem, g, w_mk, w_mv):
    batch = mem.shape[0]
    out = jax.ShapeDtypeStruct((DEPTH, batch, MEM_TOKENS, MEM_WIDTH), BF16)
    return pl.pallas_call(
        _mem_kv_kernel,
        grid=(DEPTH, batch),
        in_specs=[
            pl.BlockSpec((1, MEM_TOKENS, D_MODEL), lambda l, b: (b, 0, 0)),
            pl.BlockSpec((1, 1, D_MODEL), lambda l, b: (l, 0, 0)),
            pl.BlockSpec((1, D_MODEL, MEM_WIDTH), lambda l, b: (l, 0, 0)),
            pl.BlockSpec((1, D_MODEL, MEM_WIDTH), lambda l, b: (l, 0, 0)),
        ],
        out_specs=[
            pl.BlockSpec((1, 1, MEM_TOKENS, MEM_WIDTH), lambda l, b: (l, b, 0, 0)),
            pl.BlockSpec((1, 1, MEM_TOKENS, MEM_WIDTH), lambda l, b: (l, b, 0, 0)),
        ],
        out_shape=[out, out],
        compiler_params=pltpu.CompilerParams(dimension_semantics=("parallel", "parallel")),
        name="mem_kv",
    )(mem, g, w_mk, w_mv)


def _mix_kernel(x_ref, xp_ref, xn_ref, a_ref, mk_ref, mv_ref, gpre_ref, w_conv_ref, w_other_ref, cw_ref, y_ref,
                *, seq):
    rows = x_ref.shape[0]
    i = pl.program_id(0)
    keep_prev = ((i * rows) % seq != 0).astype(F32)
    keep_next = (((i + 1) * rows) % seq != 0).astype(F32)
    x_ext = jnp.concatenate([xp_ref[...] * keep_prev, x_ref[...], xn_ref[...] * keep_next], axis=0)
    h_ext = _rms(x_ext, gpre_ref[...]).astype(BF16)

    zc = _dot(h_ext, w_conv_ref[...])
    u = zc[:, :CONV_WIDTH] * zc[:, CONV_WIDTH:]
    ext = rows + 2 * SUBLANES
    body = slice(SUBLANES, SUBLANES + rows)
    conv = (pltpu.roll(u, 1, 0)[body] * cw_ref[0:1, :] + u[body] * cw_ref[1:2, :]
            + pltpu.roll(u, ext - 1, 0)[body] * cw_ref[2:3, :])

    zo = _dot(h_ext[body], w_other_ref[...])
    c_out = zo[:, :CONV_WIDTH] * conv
    gate = zo[:, CONV_WIDTH + MEM_WIDTH:]
    act = gate * (1.0 / (1.0 + jnp.exp(-gate)))

    y_ref[:, :MLA_WIDTH] = (a_ref[...].astype(F32) * act[:, :MLA_WIDTH]).astype(BF16)
    y_ref[:, MLA_WIDTH:MLA_WIDTH + CONV_WIDTH] = (c_out * act[:, MLA_WIDTH:MLA_WIDTH + CONV_WIDTH]).astype(BF16)

    mem_scale = MEM_HEAD_DIM ** -0.5
    for hd in range(MEM_HEADS):
        lo, hi = hd * MEM_HEAD_DIM, (hd + 1) * MEM_HEAD_DIM
        qh = (zo[:, CONV_WIDTH + lo:CONV_WIDTH + hi] * mem_scale).astype(BF16)
        s = _dot_nt(qh, mk_ref[0, :, lo:hi])
        p = jnp.exp(s - jnp.max(s, axis=-1, keepdims=True))
        o = _dot(p.astype(BF16), mv_ref[0, :, lo:hi]) * (1.0 / jnp.sum(p, axis=-1, keepdims=True))
        col = MLA_WIDTH + CONV_WIDTH + lo
        y_ref[:, col:col + MEM_HEAD_DIM] = (o * act[:, col:col + MEM_HEAD_DIM]).astype(BF16)


def _mix(x2, a2, mk, mv, layer, gpre, w_conv, w_other, cw, batch, seq):
    rows = MIX_ROWS
    tokens = batch * seq
    per_seq = seq // rows
    tiles = rows // SUBLANES
    last_tile = tokens // SUBLANES - 1
    const = lambda i: (0, 0)
    single = pl.Buffered(1)
    return pl.pallas_call(
        functools.partial(_mix_kernel, seq=seq),
        grid=(tokens // rows,),
        in_specs=[
            pl.BlockSpec((rows, D_MODEL), lambda i: (i, 0)),
            pl.BlockSpec((SUBLANES, D_MODEL), lambda i: (jnp.maximum(i * tiles - 1, 0), 0)),
            pl.BlockSpec((SUBLANES, D_MODEL), lambda i: (jnp.minimum((i + 1) * tiles, last_tile), 0)),
            pl.BlockSpec((rows, MLA_WIDTH), lambda i: (i, 0)),
            pl.BlockSpec((None, 1, MEM_TOKENS, MEM_WIDTH), lambda i: (layer, i // per_seq, 0, 0)),
            pl.BlockSpec((None, 1, MEM_TOKENS, MEM_WIDTH), lambda i: (layer, i // per_seq, 0, 0)),
            pl.BlockSpec(gpre.shape, const),
            pl.BlockSpec(w_conv.shape, const, pipeline_mode=single),
            pl.BlockSpec(w_other.shape, const, pipeline_mode=single),
            pl.BlockSpec(cw.shape, const),
        ],
        out_specs=pl.BlockSpec((rows, MIX_WIDTH), lambda i: (i, 0)),
        out_shape=jax.ShapeDtypeStruct((tokens, MIX_WIDTH), BF16),
        compiler_params=pltpu.CompilerParams(dimension_semantics=("parallel",), vmem_limit_bytes=VMEM_LIMIT),
        name="mix",
    )(x2, x2, x2, a2, mk, mv, gpre, w_conv, w_other, cw)


def _out_proj_kernel(y_ref, w_o_ref, x_ref, g_ref, o_ref):
    o = _dot(y_ref[...], w_o_ref[...])
    o_ref[...] = x_ref[...] + _rms(o, g_ref[...])


def _out_proj(y2, w_o, x2, g):
    rows = OUT_ROWS
    tokens = x2.shape[0]
    const = lambda i: (0, 0)
    return pl.pallas_call(
        _out_proj_kernel,
        grid=(tokens // rows,),
        in_specs=[
            pl.BlockSpec((rows, MIX_WIDTH), lambda i: (i, 0)),
            pl.BlockSpec(w_o.shape, const, pipeline_mode=pl.Buffered(1)),
            pl.BlockSpec((rows, D_MODEL), lambda i: (i, 0)),
            pl.BlockSpec(g.shape, const),
        ],
        out_specs=pl.BlockSpec((rows, D_MODEL), lambda i: (i, 0)),
        out_shape=jax.ShapeDtypeStruct(x2.shape, F32),
        compiler_params=pltpu.CompilerParams(dimension_semantics=("parallel",), vmem_limit_bytes=VMEM_LIMIT),
        name="out_proj",
    )(y2, w_o, x2, g)


def _rope_inv_freq_tile():
    inv_freq = 1.0 / (ROPE_THETA ** (jnp.arange(0, QK_ROPE_DIM, 2, dtype=F32) / QK_ROPE_DIM))
    return jnp.concatenate([inv_freq, inv_freq, jnp.zeros((LANES - QK_ROPE_DIM,), F32)])[None, :]


def kernel(x, mem, positions, pre_norm_g, w_in, q_norm_g, w_uq, kv_norm_g, w_ukv, conv_w, mem_norm_g, w_mk, w_mv,
           w_o, post_norm_g):
    batch, seq, _ = x.shape
    tokens = batch * seq
    assert seq % PREP_ROWS == 0 and seq % MIX_ROWS == 0 and seq % ATTN_Q == 0 and seq % ATTN_K == 0
    assert tokens % OUT_ROWS == 0

    c0 = Q_LORA_RANK + KV_LORA_RANK
    c1 = c0 + QK_ROPE_DIM
    c_gb, c_gc, c_xin, c_qm = c1, c1 + CONV_WIDTH, c1 + 2 * CONV_WIDTH, c1 + 3 * CONV_WIDTH
    c_gate = c_qm + MEM_WIDTH
    w_in_b = w_in.astype(BF16)
    w_in_mla = jnp.pad(w_in_b[:, :, :c1], ((0, 0), (0, 0), (0, LANES - QK_ROPE_DIM)))
    w_conv = w_in_b[:, :, c_gc:c_qm]
    w_other = jnp.concatenate([w_in_b[:, :, c_gb:c_gc], w_in_b[:, :, c_qm:]], axis=-1)
    del c_gate

    w_uq_h = w_uq.astype(BF16).reshape(DEPTH, Q_LORA_RANK, MLA_HEADS, QK_HEAD_DIM)
    w_uq_nope = w_uq_h[..., :QK_NOPE_DIM].reshape(DEPTH, Q_LORA_RANK, MLA_WIDTH)
    w_uq_rope = jnp.pad(w_uq_h[..., QK_NOPE_DIM:], ((0, 0), (0, 0), (0, 0), (0, LANES - QK_ROPE_DIM)))
    w_uq_p = jnp.concatenate([w_uq_nope, w_uq_rope.reshape(DEPTH, Q_LORA_RANK, MLA_HEADS * LANES)], axis=-1)

    w_ukv_h = w_ukv.astype(BF16).reshape(DEPTH, KV_LORA_RANK, MLA_HEADS, QK_NOPE_DIM + V_HEAD_DIM)
    w_k = w_ukv_h[..., :QK_NOPE_DIM].reshape(DEPTH, KV_LORA_RANK, MLA_WIDTH)
    w_vt = jnp.swapaxes(w_ukv_h[..., QK_NOPE_DIM:].reshape(DEPTH, KV_LORA_RANK, MLA_WIDTH), 1, 2)

    w_o_b = w_o.astype(BF16)
    invf = _rope_inv_freq_tile()
    pos2 = positions.reshape(tokens, 1)

    mk, mv = _mem_kv(mem, mem_norm_g[:, None, :], w_mk.astype(BF16), w_mv.astype(BF16))

    x2 = x.reshape(tokens, D_MODEL)
    for l in range(DEPTH):
        gpre = pre_norm_g[l][None, :]
        q, k, vt = _mla_prep(x2, pos2, invf, gpre, w_in_mla[l], q_norm_g[l][None, :], w_uq_p[l],
                             kv_norm_g[l][None, :], w_k[l], w_vt[l], batch, seq)
        a = _mla_attn(q, k, vt)
        y = _mix(x2, a.reshape(tokens, MLA_WIDTH), mk, mv, l, gpre, w_conv[l], w_other[l], conv_w[l], batch, seq)
        x2 = _out_proj(y, w_o_b[l], x2, post_norm_g[l][None, :])
    return x2.reshape(batch, seq, D_MODEL)
```

```python
import functools
import math

import jax
import jax.numpy as jnp
from jax import lax
from jax.experimental import pallas as pl
from jax.experimental.pallas import tpu as pltpu

D_MODEL = 2048
DEPTH = 4
MEM_TOKENS = 256
EPS = 1e-6
ROPE_THETA = 10000.0
MLA_HEADS = 8
QK_NOPE_DIM = 128
QK_ROPE_DIM = 64
QK_HEAD_DIM = QK_NOPE_DIM + QK_ROPE_DIM
V_HEAD_DIM = 128
Q_LORA_RANK = 512
KV_LORA_RANK = 256
MLA_WIDTH = MLA_HEADS * V_HEAD_DIM
CONV_WIDTH = 512
MEM_HEADS = 4
MEM_HEAD_DIM = 128
MEM_WIDTH = MEM_HEADS * MEM_HEAD_DIM
MIX_WIDTH = MLA_WIDTH + CONV_WIDTH + MEM_WIDTH

LANES = 128
SUBLANES = 8
QK_PAD_DIM = QK_NOPE_DIM + LANES

BF16 = jnp.bfloat16
F32 = jnp.float32

PREP_ROWS = 512
MIX_ROWS = 512
OUT_ROWS = 512
ATTN_Q = 256
REDUCE_WAYS = 4
VMEM_LIMIT = 56 * 1024 * 1024

_NT = (((1,), (1,)), ((), ()))


def _rms(x, g):
    return x * lax.rsqrt(jnp.mean(x * x, axis=-1, keepdims=True) + EPS) * g


def _dot(a, b):
    return jnp.dot(a, b, preferred_element_type=F32)


def _dot_nt(a, b):
    return lax.dot_general(a, b, _NT, preferred_element_type=F32)


def _mla_prep_kernel(x_ref, pos_ref, invf_ref, gpre_ref, w_in_ref, gq_ref, w_uq_ref, gkv_ref, w_k_ref,
                     w_vt_ref, q_ref, k_ref, vt_ref):
    h = _rms(x_ref[...], gpre_ref[...]).astype(BF16)
    z = _dot(h, w_in_ref[...])
    qn = _rms(z[:, :Q_LORA_RANK], gq_ref[...]).astype(BF16)
    kvn = _rms(z[:, Q_LORA_RANK:Q_LORA_RANK + KV_LORA_RANK], gkv_ref[...]).astype(BF16)
    kpe = z[:, Q_LORA_RANK + KV_LORA_RANK:]

    ang = pos_ref[...].astype(F32) * invf_ref[...]
    cos = jnp.cos(ang)
    sin = jnp.sin(ang)
    lane = lax.broadcasted_iota(jnp.int32, ang.shape, 1)
    half = QK_ROPE_DIM // 2
    sin_lo = jnp.where(lane < half, -sin, 0.0)
    sin_hi = jnp.where((lane >= half) & (lane < QK_ROPE_DIM), sin, 0.0)

    def rope(t):
        return t * cos + pltpu.roll(t, half, 1) * sin_hi + pltpu.roll(t, LANES - half, 1) * sin_lo

    q_scale = (QK_HEAD_DIM ** -0.5) * math.log2(math.e)
    q = _dot(qn, w_uq_ref[...])
    kn = _dot(kvn, w_k_ref[...])
    vt = _dot_nt(w_vt_ref[...], kvn)
    kpe_r = rope(kpe).astype(BF16)
    for hd in range(MLA_HEADS):
        lo, hi = hd * LANES, (hd + 1) * LANES
        q_ref[0, hd, :, :QK_NOPE_DIM] = (q[:, lo:hi] * q_scale).astype(BF16)
        q_ref[0, hd, :, QK_NOPE_DIM:] = (rope(q[:, MLA_WIDTH + lo:MLA_WIDTH + hi]) * q_scale).astype(BF16)
        k_ref[0, hd, :, :QK_NOPE_DIM] = kn[:, lo:hi].astype(BF16)
        k_ref[0, hd, :, QK_NOPE_DIM:] = kpe_r
        vt_ref[0, hd] = vt[lo:hi, :].astype(BF16)


def _mla_prep(x2, pos2, invf, gpre, w_in_mla, gq, w_uq_p, gkv, w_k, w_vt, batch, seq):
    rows = PREP_ROWS
    per_seq = seq // rows
    const = lambda i: (0, 0)
    head_rows = lambda i: (i // per_seq, 0, i % per_seq, 0)
    return pl.pallas_call(
        _mla_prep_kernel,
        grid=(batch * seq // rows,),
        in_specs=[
            pl.BlockSpec((rows, D_MODEL), lambda i: (i, 0)),
            pl.BlockSpec((rows, 1), lambda i: (i, 0)),
            pl.BlockSpec(invf.shape, const),
            pl.BlockSpec(gpre.shape, const),
            pl.BlockSpec(w_in_mla.shape, const),
            pl.BlockSpec(gq.shape, const),
            pl.BlockSpec(w_uq_p.shape, const),
            pl.BlockSpec(gkv.shape, const),
            pl.BlockSpec(w_k.shape, const),
            pl.BlockSpec(w_vt.shape, const),
        ],
        out_specs=[
            pl.BlockSpec((1, MLA_HEADS, rows, QK_PAD_DIM), head_rows),
            pl.BlockSpec((1, MLA_HEADS, rows, QK_PAD_DIM), head_rows),
            pl.BlockSpec((1, MLA_HEADS, V_HEAD_DIM, rows), lambda i: (i // per_seq, 0, 0, i % per_seq)),
        ],
        out_shape=[
            jax.ShapeDtypeStruct((batch, MLA_HEADS, seq, QK_PAD_DIM), BF16),
            jax.ShapeDtypeStruct((batch, MLA_HEADS, seq, QK_PAD_DIM), BF16),
            jax.ShapeDtypeStruct((batch, MLA_HEADS, V_HEAD_DIM, seq), BF16),
        ],
        compiler_params=pltpu.CompilerParams(dimension_semantics=("parallel",), vmem_limit_bytes=VMEM_LIMIT),
        name="mla_prep",
    )(x2, pos2, invf, gpre, w_in_mla, gq, w_uq_p, gkv, w_k, w_vt)


def _mla_attn_kernel(q_ref, k_ref, vt_ref, o_ref, s_a, s_b, *, seq):
    n_blocks = seq // ATTN_Q
    assert n_blocks % 2 == 0
    bufs = (s_a, s_b)

    def col_reduce(op, t):
        parts = op(t.reshape(REDUCE_WAYS, seq // REDUCE_WAYS, ATTN_Q), axis=1)
        return op(parts, axis=0, keepdims=True)

    def scores(j, buf):
        start = pl.multiple_of(j * ATTN_Q, ATTN_Q)
        s = _dot_nt(k_ref[0, 0], q_ref[0, 0, pl.ds(start, ATTN_Q), :])
        buf[...] = s
        return col_reduce(jnp.max, s)

    def finish(j, buf, m):
        p = jnp.exp2(buf[...] - m)
        l = col_reduce(jnp.sum, p)
        acc = _dot(vt_ref[0, 0], p.astype(BF16))
        start = pl.multiple_of(j * ATTN_Q, ATTN_Q)
        o_ref[0, pl.ds(start, ATTN_Q), :] = (acc * (1.0 / l)).T.astype(o_ref.dtype)

    def step(j, m_prev):
        def run(jj):
            m = scores(jj, bufs[j % 2])
            finish(jj - 1, bufs[(j - 1) % 2], m_prev)
            return m
        return run

    def pair(t, m_prev):
        m = step(1, m_prev)(2 * t + 1)
        return step(2, m)(2 * t + 2)

    m = lax.fori_loop(0, n_blocks // 2 - 1, pair, scores(0, bufs[0]))
    m = step(n_blocks - 1, m)(n_blocks - 1)
    finish(n_blocks - 1, bufs[(n_blocks - 1) % 2], m)


def _mla_attn(q, k, vt):
    batch, heads, seq, _ = q.shape
    return pl.pallas_call(
        functools.partial(_mla_attn_kernel, seq=seq),
        grid=(batch, heads),
        in_specs=[
            pl.BlockSpec((1, 1, seq, QK_PAD_DIM), lambda b, h: (b, h, 0, 0)),
            pl.BlockSpec((1, 1, seq, QK_PAD_DIM), lambda b, h: (b, h, 0, 0)),
            pl.BlockSpec((1, 1, V_HEAD_DIM, seq), lambda b, h: (b, h, 0, 0)),
        ],
        out_specs=pl.BlockSpec((1, seq, V_HEAD_DIM), lambda b, h: (b, 0, h)),
        out_shape=jax.ShapeDtypeStruct((batch, seq, MLA_WIDTH), BF16),
        scratch_shapes=[pltpu.VMEM((seq, ATTN_Q), F32), pltpu.VMEM((seq, ATTN_Q), F32)],
        compiler_params=pltpu.CompilerParams(
            dimension_semantics=("parallel", "parallel"), vmem_limit_bytes=VMEM_LIMIT),
        name="mla_attn",
    )(q, k, vt)


def _mem_kv_kernel(mem_ref, g_ref, w_mk_ref, w_mv_ref, mk_ref, mv_ref):
    mem_n = _rms(mem_ref[0], g_ref[0]).astype(BF16)
    mk_ref[0, 0] = _dot(mem_n, w_mk_ref[0]).astype(BF16)
    mv_ref[0, 0] = _dot(mem_n, w_mv_ref[0]).astype(BF16)


def _mem_kv(mem, g, w_mk, w_mv):
    batch = mem.shape[0]
    out = jax.ShapeDtypeStruct((DEPTH, batch, MEM_TOKENS, MEM_WIDTH), BF16)
    return pl.pallas_call(
        _mem_kv_kernel,
        grid=(DEPTH, batch),
        in_specs=[
            pl.BlockSpec((1, MEM_TOKENS, D_MODEL), lambda l, b: (b, 0, 0)),
            pl.BlockSpec((1, 1, D_MODEL), lambda l, b: (l, 0, 0)),
            pl.BlockSpec((1, D_MODEL, MEM_WIDTH), lambda l, b: (l, 0, 0)),
            pl.BlockSpec((1, D_MODEL, MEM_WIDTH), lambda l, b: (l, 0, 0)),
        ],
        out_specs=[
            pl.BlockSpec((1, 1, MEM_TOKENS, MEM_WIDTH), lambda l, b: (l, b, 0, 0)),
            pl.BlockSpec((1, 1, MEM_TOKENS, MEM_WIDTH), lambda l, b: (l, b, 0, 0)),
        ],
        out_shape=[out, out],
        compiler_params=pltpu.CompilerParams(dimension_semantics=("parallel", "parallel")),
        name="mem_kv",
    )(mem, g, w_mk, w_mv)


def _mix_kernel(x_ref, xp_ref, xn_ref, a_ref, mk_ref, mv_ref, gpre_ref, w_conv_ref, w_other_ref, cw_ref, y_ref,
                *, seq):
    rows = x_ref.shape[0]
    i = pl.program_id(0)
    keep_prev = ((i * rows) % seq != 0).astype(F32)
    keep_next = (((i + 1) * rows) % seq != 0).astype(F32)
    x_ext = jnp.concatenate([xp_ref[...] * keep_prev, x_ref[...], xn_ref[...] * keep_next], axis=0)
    h_ext = _rms(x_ext, gpre_ref[...]).astype(BF16)

    zc = _dot(h_ext, w_conv_ref[...])
    u = zc[:, :CONV_WIDTH] * zc[:, CONV_WIDTH:]
    ext = rows + 2 * SUBLANES
    body = slice(SUBLANES, SUBLANES + rows)
    conv = (pltpu.roll(u, 1, 0)[body] * cw_ref[0:1, :] + u[body] * cw_ref[1:2, :]
            + pltpu.roll(u, ext - 1, 0)[body] * cw_ref[2:3, :])

    zo = _dot(h_ext[body], w_other_ref[...])
    c_out = zo[:, :CONV_WIDTH] * conv
    gate = zo[:, CONV_WIDTH + MEM_WIDTH:]
    act = gate * (1.0 / (1.0 + jnp.exp(-gate)))

    y_ref[:, :MLA_WIDTH] = (a_ref[...].astype(F32) * act[:, :MLA_WIDTH]).astype(BF16)
    y_ref[:, MLA_WIDTH:MLA_WIDTH + CONV_WIDTH] = (c_out * act[:, MLA_WIDTH:MLA_WIDTH + CONV_WIDTH]).astype(BF16)

    mem_scale = MEM_HEAD_DIM ** -0.5
    for hd in range(MEM_HEADS):
        lo, hi = hd * MEM_HEAD_DIM, (hd + 1) * MEM_HEAD_DIM
        qh = (zo[:, CONV_WIDTH + lo:CONV_WIDTH + hi] * mem_scale).astype(BF16)
        s = _dot_nt(qh, mk_ref[0, :, lo:hi])
        p = jnp.exp(s - jnp.max(s, axis=-1, keepdims=True))
        o = _dot(p.astype(BF16), mv_ref[0, :, lo:hi]) * (1.0 / jnp.sum(p, axis=-1, keepdims=True))
        col = MLA_WIDTH + CONV_WIDTH + lo
        y_ref[:, col:col + MEM_HEAD_DIM] = (o * act[:, col:col + MEM_HEAD_DIM]).astype(BF16)


def _mix(x2, a2, mk, mv, layer, gpre, w_conv, w_other, cw, batch, seq):
    rows = MIX_ROWS
    tokens = batch * seq
    per_seq = seq // rows
    tiles = rows // SUBLANES
    last_tile = tokens // SUBLANES - 1
    const = lambda i: (0, 0)
    single = pl.Buffered(1)
    return pl.pallas_call(
        functools.partial(_mix_kernel, seq=seq),
        grid=(tokens // rows,),
        in_specs=[
            pl.BlockSpec((rows, D_MODEL), lambda i: (i, 0)),
            pl.BlockSpec((SUBLANES, D_MODEL), lambda i: (jnp.maximum(i * tiles - 1, 0), 0)),
            pl.BlockSpec((SUBLANES, D_MODEL), lambda i: (jnp.minimum((i + 1) * tiles, last_tile), 0)),
            pl.BlockSpec((rows, MLA_WIDTH), lambda i: (i, 0)),
            pl.BlockSpec((None, 1, MEM_TOKENS, MEM_WIDTH), lambda i: (layer, i // per_seq, 0, 0)),
            pl.BlockSpec((None, 1, MEM_TOKENS, MEM_WIDTH), lambda i: (layer, i // per_seq, 0, 0)),
            pl.BlockSpec(gpre.shape, const),
            pl.BlockSpec(w_conv.shape, const, pipeline_mode=single),
            pl.BlockSpec(w_other.shape, const, pipeline_mode=single),
            pl.BlockSpec(cw.shape, const),
        ],
        out_specs=pl.BlockSpec((rows, MIX_WIDTH), lambda i: (i, 0)),
        out_shape=jax.ShapeDtypeStruct((tokens, MIX_WIDTH), BF16),
        compiler_params=pltpu.CompilerParams(dimension_semantics=("parallel",), vmem_limit_bytes=VMEM_LIMIT),
        name="mix",
    )(x2, x2, x2, a2, mk, mv, gpre, w_conv, w_other, cw)


def _out_proj_kernel(y_ref, w_o_ref, x_ref, g_ref, o_ref):
    o = _dot(y_ref[...], w_o_ref[...])
    o_ref[...] = x_ref[...] + _rms(o, g_ref[...])


def _out_proj(y2, w_o, x2, g):
    rows = OUT_ROWS
    tokens = x2.shape[0]
    const = lambda i: (0, 0)
    return pl.pallas_call(
        _out_proj_kernel,
        grid=(tokens // rows,),
        in_specs=[
            pl.BlockSpec((rows, MIX_WIDTH), lambda i: (i, 0)),
            pl.BlockSpec(w_o.shape, const, pipeline_mode=pl.Buffered(1)),
            pl.BlockSpec((rows, D_MODEL), lambda i: (i, 0)),
            pl.BlockSpec(g.shape, const),
        ],
        out_specs=pl.BlockSpec((rows, D_MODEL), lambda i: (i, 0)),
        out_shape=jax.ShapeDtypeStruct(x2.shape, F32),
        compiler_params=pltpu.CompilerParams(dimension_semantics=("parallel",), vmem_limit_bytes=VMEM_LIMIT),
        name="out_proj",
    )(y2, w_o, x2, g)


def _rope_inv_freq_tile():
    inv_freq = 1.0 / (ROPE_THETA ** (jnp.arange(0, QK_ROPE_DIM, 2, dtype=F32) / QK_ROPE_DIM))
    return jnp.concatenate([inv_freq, inv_freq, jnp.zeros((LANES - QK_ROPE_DIM,), F32)])[None, :]


def kernel(x, mem, positions, pre_norm_g, w_in, q_norm_g, w_uq, kv_norm_g, w_ukv, conv_w, mem_norm_g, w_mk, w_mv,
           w_o, post_norm_g):
    batch, seq, _ = x.shape
    tokens = batch * seq
    assert seq % PREP_ROWS == 0 and seq % MIX_ROWS == 0 and seq % ATTN_Q == 0
    assert tokens % OUT_ROWS == 0

    c0 = Q_LORA_RANK + KV_LORA_RANK
    c1 = c0 + QK_ROPE_DIM
    c_gb, c_gc, c_xin, c_qm = c1, c1 + CONV_WIDTH, c1 + 2 * CONV_WIDTH, c1 + 3 * CONV_WIDTH
    c_gate = c_qm + MEM_WIDTH
    w_in_b = w_in.astype(BF16)
    w_in_mla = jnp.pad(w_in_b[:, :, :c1], ((0, 0), (0, 0), (0, LANES - QK_ROPE_DIM)))
    w_conv = w_in_b[:, :, c_gc:c_qm]
    w_other = jnp.concatenate([w_in_b[:, :, c_gb:c_gc], w_in_b[:, :, c_qm:]], axis=-1)
    del c_gate

    w_uq_h = w_uq.astype(BF16).reshape(DEPTH, Q_LORA_RANK, MLA_HEADS, QK_HEAD_DIM)
    w_uq_nope = w_uq_h[..., :QK_NOPE_DIM].reshape(DEPTH, Q_LORA_RANK, MLA_WIDTH)
    w_uq_rope = jnp.pad(w_uq_h[..., QK_NOPE_DIM:], ((0, 0), (0, 0), (0, 0), (0, LANES - QK_ROPE_DIM)))
    w_uq_p = jnp.concatenate([w_uq_nope, w_uq_rope.reshape(DEPTH, Q_LORA_RANK, MLA_HEADS * LANES)], axis=-1)

    w_ukv_h = w_ukv.astype(BF16).reshape(DEPTH, KV_LORA_RANK, MLA_HEADS, QK_NOPE_DIM + V_HEAD_DIM)
    w_k = w_ukv_h[..., :QK_NOPE_DIM].reshape(DEPTH, KV_LORA_RANK, MLA_WIDTH)
    w_vt = jnp.swapaxes(w_ukv_h[..., QK_NOPE_DIM:].reshape(DEPTH, KV_LORA_RANK, MLA_WIDTH), 1, 2)

    w_o_b = w_o.astype(BF16)
    invf = _rope_inv_freq_tile()
    pos2 = positions.reshape(tokens, 1)

    mk, mv = _mem_kv(mem, mem_norm_g[:, None, :], w_mk.astype(BF16), w_mv.astype(BF16))

    x2 = x.reshape(tokens, D_MODEL)
    for l in range(DEPTH):
        gpre = pre_norm_g[l][None, :]
        q, k, vt = _mla_prep(x2, pos2, invf, gpre, w_in_mla[l], q_norm_g[l][None, :], w_uq_p[l],
                             kv_norm_g[l][None, :], w_k[l], w_vt[l], batch, seq)
        a = _mla_attn(q, k, vt)
        y = _mix(x2, a.reshape(tokens, MLA_WIDTH), mk, mv, l, gpre, w_conv[l], w_other[l], conv_w[l], batch, seq)
        x2 = _out_proj(y, w_o_b[l], x2, post_norm_g[l][None, :])
    return x2.reshape(batch, seq, D_MODEL)
```

```python
import functools
import math

import jax
import jax.numpy as jnp
from jax import lax
from jax.experimental import pallas as pl
from jax.experimental.pallas import tpu as pltpu

D_MODEL = 2048
DEPTH = 4
MEM_TOKENS = 256
EPS = 1e-6
ROPE_THETA = 10000.0
MLA_HEADS = 8
QK_NOPE_DIM = 128
QK_ROPE_DIM = 64
QK_HEAD_DIM = QK_NOPE_DIM + QK_ROPE_DIM
V_HEAD_DIM = 128
Q_LORA_RANK = 512
KV_LORA_RANK = 256
MLA_WIDTH = MLA_HEADS * V_HEAD_DIM
CONV_WIDTH = 512
MEM_HEADS = 4
MEM_HEAD_DIM = 128
MEM_WIDTH = MEM_HEADS * MEM_HEAD_DIM
MIX_WIDTH = MLA_WIDTH + CONV_WIDTH + MEM_WIDTH

LANES = 128
SUBLANES = 8
QK_PAD_DIM = QK_NOPE_DIM + LANES
ONES_ROWS = 2 * SUBLANES
VT_ROWS = V_HEAD_DIM + ONES_ROWS

BF16 = jnp.bfloat16
F32 = jnp.float32

PREP_ROWS = 512
MIX_ROWS = 512
OUT_ROWS = 512
ATTN_Q = 256
ATTN_K = 512
VMEM_LIMIT = 56 * 1024 * 1024

_NT = (((1,), (1,)), ((), ()))


def _rms(x, g):
    return x * lax.rsqrt(jnp.mean(x * x, axis=-1, keepdims=True) + EPS) * g


def _dot(a, b):
    return jnp.dot(a, b, preferred_element_type=F32)


def _dot_nt(a, b):
    return lax.dot_general(a, b, _NT, preferred_element_type=F32)


def _mla_prep_kernel(x_ref, pos_ref, invf_ref, gpre_ref, w_in_ref, gq_ref, w_uqt_ref, gkv_ref, w_k_ref,
                     w_vt_ref, qt_ref, k_ref, vt_ref):
    h = _rms(x_ref[...], gpre_ref[...]).astype(BF16)
    z = _dot(h, w_in_ref[...])
    qn = _rms(z[:, :Q_LORA_RANK], gq_ref[...]).astype(BF16)
    kvn = _rms(z[:, Q_LORA_RANK:Q_LORA_RANK + KV_LORA_RANK], gkv_ref[...]).astype(BF16)
    kpe = z[:, Q_LORA_RANK + KV_LORA_RANK:]

    ang = pos_ref[...].astype(F32) * invf_ref[...]
    cos = jnp.cos(ang)
    sin = jnp.sin(ang)
    lane = lax.broadcasted_iota(jnp.int32, ang.shape, 1)
    half = QK_ROPE_DIM // 2
    sin_lo = jnp.where(lane < half, -sin, 0.0)
    sin_hi = jnp.where((lane >= half) & (lane < QK_ROPE_DIM), sin, 0.0)

    def rope(t):
        return t * cos + pltpu.roll(t, half, 1) * sin_hi + pltpu.roll(t, LANES - half, 1) * sin_lo

    cos_t = cos.T[:half]
    sin_t = sin.T[:half]

    q_scale = (QK_HEAD_DIM ** -0.5) * math.log2(math.e)
    q_t = _dot_nt(w_uqt_ref[...], qn) * q_scale
    kn = _dot(kvn, w_k_ref[...])
    vt = _dot_nt(w_vt_ref[...], kvn)
    kpe_r = rope(kpe).astype(BF16)
    pad_rows = QK_PAD_DIM - QK_HEAD_DIM
    for hd in range(MLA_HEADS):
        lo, hi = hd * LANES, (hd + 1) * LANES
        x1 = q_t[MLA_WIDTH + lo:MLA_WIDTH + lo + half]
        x2 = q_t[MLA_WIDTH + lo + half:MLA_WIDTH + lo + QK_ROPE_DIM]
        qt_ref[0, hd, :QK_NOPE_DIM, :] = q_t[lo:hi].astype(BF16)
        qt_ref[0, hd, QK_NOPE_DIM:QK_NOPE_DIM + half, :] = (x1 * cos_t - x2 * sin_t).astype(BF16)
        qt_ref[0, hd, QK_NOPE_DIM + half:QK_HEAD_DIM, :] = (x2 * cos_t + x1 * sin_t).astype(BF16)
        qt_ref[0, hd, QK_HEAD_DIM:, :] = jnp.zeros((pad_rows, q_t.shape[1]), BF16)
        k_ref[0, hd, :, :QK_NOPE_DIM] = kn[:, lo:hi].astype(BF16)
        k_ref[0, hd, :, QK_NOPE_DIM:] = kpe_r
        vt_ref[0, hd, :V_HEAD_DIM, :] = vt[lo:hi, :].astype(BF16)
        vt_ref[0, hd, V_HEAD_DIM:, :] = jnp.ones((ONES_ROWS, vt.shape[1]), BF16)


def _mla_prep(x2, pos2, invf, gpre, w_in_mla, gq, w_uq_p, gkv, w_k, w_vt, batch, seq):
    rows = PREP_ROWS
    per_seq = seq // rows
    const = lambda i: (0, 0)
    head_rows = lambda i: (i // per_seq, 0, i % per_seq, 0)
    head_cols = lambda i: (i // per_seq, 0, 0, i % per_seq)
    return pl.pallas_call(
        _mla_prep_kernel,
        grid=(batch * seq // rows,),
        in_specs=[
            pl.BlockSpec((rows, D_MODEL), lambda i: (i, 0)),
            pl.BlockSpec((rows, 1), lambda i: (i, 0)),
            pl.BlockSpec(invf.shape, const),
            pl.BlockSpec(gpre.shape, const),
            pl.BlockSpec(w_in_mla.shape, const),
            pl.BlockSpec(gq.shape, const),
            pl.BlockSpec(w_uq_p.shape, const),
            pl.BlockSpec(gkv.shape, const),
            pl.BlockSpec(w_k.shape, const),
            pl.BlockSpec(w_vt.shape, const),
        ],
        out_specs=[
            pl.BlockSpec((1, MLA_HEADS, QK_PAD_DIM, rows), head_cols),
            pl.BlockSpec((1, MLA_HEADS, rows, QK_PAD_DIM), head_rows),
            pl.BlockSpec((1, MLA_HEADS, VT_ROWS, rows), head_cols),
        ],
        out_shape=[
            jax.ShapeDtypeStruct((batch, MLA_HEADS, QK_PAD_DIM, seq), BF16),
            jax.ShapeDtypeStruct((batch, MLA_HEADS, seq, QK_PAD_DIM), BF16),
            jax.ShapeDtypeStruct((batch, MLA_HEADS, VT_ROWS, seq), BF16),
        ],
        compiler_params=pltpu.CompilerParams(dimension_semantics=("parallel",), vmem_limit_bytes=VMEM_LIMIT),
        name="mla_prep",
    )(x2, pos2, invf, gpre, w_in_mla, gq, w_uq_p, gkv, w_k, w_vt)


def _mla_attn_kernel(qt_ref, k_ref, vt_ref, o_ref, s_buf, *, seq):
    n_blocks = seq // ATTN_Q
    n_chunks = seq // ATTN_K

    def q_block(j):
        return qt_ref[0, 0, :, pl.ds(pl.multiple_of(j * ATTN_Q, ATTN_Q), ATTN_Q)]

    def read_probs(c, m):
        return jnp.exp2((s_buf[c * ATTN_K:(c + 1) * ATTN_K, :] - m).astype(BF16))

    def write_scores(c, q_t):
        s = _dot(k_ref[0, 0, c * ATTN_K:(c + 1) * ATTN_K, :], q_t)
        s_buf[c * ATTN_K:(c + 1) * ATTN_K, :] = s
        return jnp.max(s, axis=0, keepdims=True)

    def weighted_values(probs):
        return _dot(vt_ref[0, 0], jnp.concatenate(probs, axis=0))

    def emit(j, acc):
        o_t = acc[:V_HEAD_DIM] * (1.0 / acc[V_HEAD_DIM:V_HEAD_DIM + 1])
        o_ref[0, pl.ds(pl.multiple_of(j * ATTN_Q, ATTN_Q), ATTN_Q), :] = o_t.T.astype(o_ref.dtype)

    def col_max(parts):
        return functools.reduce(jnp.maximum, parts)

    def step(j, carry):
        m_prev, acc_prev = carry
        emit(jnp.maximum(j - 2, 0), acc_prev)
        q_t = q_block(j)
        probs, maxes = [], []
        for c in range(n_chunks):
            probs.append(read_probs(c, m_prev))
            maxes.append(write_scores(c, q_t))
        return col_max(maxes), weighted_values(probs)

    q0 = q_block(0)
    m0 = col_max([write_scores(c, q0) for c in range(n_chunks)])
    m, acc = lax.fori_loop(1, n_blocks, step, (m0, jnp.ones((VT_ROWS, ATTN_Q), F32)))
    emit(n_blocks - 2, acc)
    emit(n_blocks - 1, weighted_values([read_probs(c, m) for c in range(n_chunks)]))


def _mla_attn(q_t, k, vt):
    batch, heads, seq, _ = k.shape
    return pl.pallas_call(
        functools.partial(_mla_attn_kernel, seq=seq),
        grid=(batch, heads),
        in_specs=[
            pl.BlockSpec((1, 1, QK_PAD_DIM, seq), lambda b, h: (b, h, 0, 0)),
            pl.BlockSpec((1, 1, seq, QK_PAD_DIM), lambda b, h: (b, h, 0, 0)),
            pl.BlockSpec((1, 1, VT_ROWS, seq), lambda b, h: (b, h, 0, 0)),
        ],
        out_specs=pl.BlockSpec((1, seq, V_HEAD_DIM), lambda b, h: (b, 0, h)),
        out_shape=jax.ShapeDtypeStruct((batch, seq, MLA_WIDTH), BF16),
        scratch_shapes=[pltpu.VMEM((seq, ATTN_Q), F32)],
        compiler_params=pltpu.CompilerParams(
            dimension_semantics=("parallel", "parallel"), vmem_limit_bytes=VMEM_LIMIT),
        name="mla_attn",
    )(q_t, k, vt)


def _mem_kv_kernel(mem_ref, g_ref, w_mk_ref, w_mv_ref, mk_ref, mv_ref):
    mem_n = _rms(mem_ref[0], g_ref[0]).astype(BF16)
    mk_ref[0, 0] = _dot(mem_n, w_mk_ref[0]).astype(BF16)
    mv_ref[0, 0] = _dot(mem_n, w_mv_ref[0]).astype(BF16)


def _mem_kv(mem, g, w_mk, w_mv):
    batch = mem.shape[0]
    out = jax.ShapeDtypeStruct((DEPTH, batch, MEM_TOKENS, MEM_WIDTH), BF16)
    return pl.pallas_call(
        _mem_kv_kernel,
        grid=(DEPTH, batch),
        in_specs=[
            pl.BlockSpec((1, MEM_TOKENS, D_MODEL), lambda l, b: (b, 0, 0)),
            pl.BlockSpec((1, 1, D_MODEL), lambda l, b: (l, 0, 0)),
            pl.BlockSpec((1, D_MODEL, MEM_WIDTH), lambda l, b: (l, 0, 0)),
            pl.BlockSpec((1, D_MODEL, MEM_WIDTH), lambda l, b: (l, 0, 0)),
        ],
        out_specs=[
            pl.BlockSpec((1, 1, MEM_TOKENS, MEM_WIDTH), lambda l, b: (l, b, 0, 0)),
            pl.BlockSpec((1, 1, MEM_TOKENS, MEM_WIDTH), lambda l, b: (l, b, 0, 0)),
        ],
        out_shape=[out, out],
        compiler_params=pltpu.CompilerParams(dimension_semantics=("parallel", "parallel")),
        name="mem_kv",
    )(mem, g, w_mk, w_mv)


def _mix_kernel(x_ref, xp_ref, xn_ref, a_ref, mk_ref, mv_ref, gpre_ref, w_conv_ref, w_other_ref, cw_ref, y_ref,
                *, seq):
    rows = x_ref.shape[0]
    i = pl.program_id(0)
    keep_prev = ((i * rows) % seq != 0).astype(F32)
    keep_next = (((i + 1) * rows) % seq != 0).astype(F32)
    x_ext = jnp.concatenate([xp_ref[...] * keep_prev, x_ref[...], xn_ref[...] * keep_next], axis=0)
    h_ext = _rms(x_ext, gpre_ref[...]).astype(BF16)

    zc = _dot(h_ext, w_conv_ref[...])
    u = zc[:, :CONV_WIDTH] * zc[:, CONV_WIDTH:]
    ext = rows + 2 * SUBLANES
    body = slice(SUBLANES, SUBLANES + rows)
    conv = (pltpu.roll(u, 1, 0)[body] * cw_ref[0:1, :] + u[body] * cw_ref[1:2, :]
            + pltpu.roll(u, ext - 1, 0)[body] * cw_ref[2:3, :])

    zo = _dot(h_ext[body], w_other_ref[...])
    c_out = zo[:, :CONV_WIDTH] * conv
    gate = zo[:, CONV_WIDTH + MEM_WIDTH:]
    act = gate * (1.0 / (1.0 + jnp.exp(-gate)))

    y_ref[:, :MLA_WIDTH] = (a_ref[...].astype(F32) * act[:, :MLA_WIDTH]).astype(BF16)
    y_ref[:, MLA_WIDTH:MLA_WIDTH + CONV_WIDTH] = (c_out * act[:, MLA_WIDTH:MLA_WIDTH + CONV_WIDTH]).astype(BF16)

    mem_scale = MEM_HEAD_DIM ** -0.5
    for hd in range(MEM_HEADS):
        lo, hi = hd * MEM_HEAD_DIM, (hd + 1) * MEM_HEAD_DIM
        qh = (zo[:, CONV_WIDTH + lo:CONV_WIDTH + hi] * mem_scale).astype(BF16)
        s = _dot_nt(qh, mk_ref[0, :, lo:hi])
        p = jnp.exp(s - jnp.max(s, axis=-1, keepdims=True))
        o = _dot(p.astype(BF16), mv_ref[0, :, lo:hi]) * (1.0 / jnp.sum(p, axis=-1, keepdims=True))
        col = MLA_WIDTH + CONV_WIDTH + lo
        y_ref[:, col:col + MEM_HEAD_DIM] = (o * act[:, col:col + MEM_HEAD_DIM]).astype(BF16)


def _mix(x2, a2, mk, mv, layer, gpre, w_conv, w_other, cw, batch, seq):
    rows = MIX_ROWS
    tokens = batch * seq
    per_seq = seq // rows
    tiles = rows // SUBLANES
    last_tile = tokens // SUBLANES - 1
    const = lambda i: (0, 0)
    single = pl.Buffered(1)
    return pl.pallas_call(
        functools.partial(_mix_kernel, seq=seq),
        grid=(tokens // rows,),
        in_specs=[
            pl.BlockSpec((rows, D_MODEL), lambda i: (i, 0)),
            pl.BlockSpec((SUBLANES, D_MODEL), lambda i: (jnp.maximum(i * tiles - 1, 0), 0)),
            pl.BlockSpec((SUBLANES, D_MODEL), lambda i: (jnp.minimum((i + 1) * tiles, last_tile), 0)),
            pl.BlockSpec((rows, MLA_WIDTH), lambda i: (i, 0)),
            pl.BlockSpec((None, 1, MEM_TOKENS, MEM_WIDTH), lambda i: (layer, i // per_seq, 0, 0)),
            pl.BlockSpec((None, 1, MEM_TOKENS, MEM_WIDTH), lambda i: (layer, i // per_seq, 0, 0)),
            pl.BlockSpec(gpre.shape, const),
            pl.BlockSpec(w_conv.shape, const, pipeline_mode=single),
            pl.BlockSpec(w_other.shape, const, pipeline_mode=single),
            pl.BlockSpec(cw.shape, const),
        ],
        out_specs=pl.BlockSpec((rows, MIX_WIDTH), lambda i: (i, 0)),
        out_shape=jax.ShapeDtypeStruct((tokens, MIX_WIDTH), BF16),
        compiler_params=pltpu.CompilerParams(dimension_semantics=("parallel",), vmem_limit_bytes=VMEM_LIMIT),
        name="mix",
    )(x2, x2, x2, a2, mk, mv, gpre, w_conv, w_other, cw)


def _out_proj_kernel(y_ref, w_o_ref, x_ref, g_ref, o_ref):
    o = _dot(y_ref[...], w_o_ref[...])
    o_ref[...] = x_ref[...] + _rms(o, g_ref[...])


def _out_proj(y2, w_o, x2, g):
    rows = OUT_ROWS
    tokens = x2.shape[0]
    const = lambda i: (0, 0)
    return pl.pallas_call(
        _out_proj_kernel,
        grid=(tokens // rows,),
        in_specs=[
            pl.BlockSpec((rows, MIX_WIDTH), lambda i: (i, 0)),
            pl.BlockSpec(w_o.shape, const, pipeline_mode=pl.Buffered(1)),
            pl.BlockSpec((rows, D_MODEL), lambda i: (i, 0)),
            pl.BlockSpec(g.shape, const),
        ],
        out_specs=pl.BlockSpec((rows, D_MODEL), lambda i: (i, 0)),
        out_shape=jax.ShapeDtypeStruct(x2.shape, F32),
        compiler_params=pltpu.CompilerParams(dimension_semantics=("parallel",), vmem_limit_bytes=VMEM_LIMIT),
        name="out_proj",
    )(y2, w_o, x2, g)


def _rope_inv_freq_tile():
    inv_freq = 1.0 / (ROPE_THETA ** (jnp.arange(0, QK_ROPE_DIM, 2, dtype=F32) / QK_ROPE_DIM))
    return jnp.concatenate([inv_freq, inv_freq, jnp.zeros((LANES - QK_ROPE_DIM,), F32)])[None, :]


def kernel(x, mem, positions, pre_norm_g, w_in, q_norm_g, w_uq, kv_norm_g, w_ukv, conv_w, mem_norm_g, w_mk, w_mv,
           w_o, post_norm_g):
    batch, seq, _ = x.shape
    tokens = batch * seq
    assert seq % PREP_ROWS == 0 and seq % MIX_ROWS == 0 and seq % ATTN_Q == 0 and seq % ATTN_K == 0
    assert tokens % OUT_ROWS == 0

    c0 = Q_LORA_RANK + KV_LORA_RANK
    c1 = c0 + QK_ROPE_DIM
    c_gb, c_gc, c_xin, c_qm = c1, c1 + CONV_WIDTH, c1 + 2 * CONV_WIDTH, c1 + 3 * CONV_WIDTH
    c_gate = c_qm + MEM_WIDTH
    w_in_b = w_in.astype(BF16)
    w_in_mla = jnp.pad(w_in_b[:, :, :c1], ((0, 0), (0, 0), (0, LANES - QK_ROPE_DIM)))
    w_conv = w_in_b[:, :, c_gc:c_qm]
    w_other = jnp.concatenate([w_in_b[:, :, c_gb:c_gc], w_in_b[:, :, c_qm:]], axis=-1)
    del c_gate

    w_uq_h = w_uq.astype(BF16).reshape(DEPTH, Q_LORA_RANK, MLA_HEADS, QK_HEAD_DIM)
    w_uq_nope = w_uq_h[..., :QK_NOPE_DIM].reshape(DEPTH, Q_LORA_RANK, MLA_WIDTH)
    w_uq_rope = jnp.pad(w_uq_h[..., QK_NOPE_DIM:], ((0, 0), (0, 0), (0, 0), (0, LANES - QK_ROPE_DIM)))
    w_uq_p = jnp.concatenate([w_uq_nope, w_uq_rope.reshape(DEPTH, Q_LORA_RANK, MLA_HEADS * LANES)], axis=-1)
    w_uq_p = jnp.swapaxes(w_uq_p, 1, 2)

    w_ukv_h = w_ukv.astype(BF16).reshape(DEPTH, KV_LORA_RANK, MLA_HEADS, QK_NOPE_DIM + V_HEAD_DIM)
    w_k = w_ukv_h[..., :QK_NOPE_DIM].reshape(DEPTH, KV_LORA_RANK, MLA_WIDTH)
    w_vt = jnp.swapaxes(w_ukv_h[..., QK_NOPE_DIM:].reshape(DEPTH, KV_LORA_RANK, MLA_WIDTH), 1, 2)

    w_o_b = w_o.astype(BF16)
    invf = _rope_inv_freq_tile()
    pos2 = positions.reshape(tokens, 1)

    mk, mv = _mem_kv(mem, mem_norm_g[:, None, :], w_mk.astype(BF16), w_mv.astype(BF16))

    x2 = x.reshape(tokens, D_MODEL)
    for l in range(DEPTH):
        gpre = pre_norm_g[l][None, :]
        q, k, vt = _mla_prep(x2, pos2, invf, gpre, w_in_mla[l], q_norm_g[l][None, :], w_uq_p[l],
                             kv_norm_g[l][None, :], w_k[l], w_vt[l], batch, seq)
        a = _mla_attn(q, k, vt)
        y = _mix(x2, a.reshape(tokens, MLA_WIDTH), mk, mv, l, gpre, w_conv[l], w_other[l], conv_w[l], batch, seq)
        x2 = _out_proj(y, w_o_b[l], x2, post_norm_g[l][None, :])
    return x2.reshape(batch, seq, D_MODEL)
```

```python
import functools
import math

import jax
import jax.numpy as jnp
from jax import lax
from jax.experimental import pallas as pl
from jax.experimental.pallas import tpu as pltpu

D_MODEL = 2048
DEPTH = 4
MEM_TOKENS = 256
EPS = 1e-6
ROPE_THETA = 10000.0
MLA_HEADS = 8
QK_NOPE_DIM = 128
QK_ROPE_DIM = 64
QK_HEAD_DIM = QK_NOPE_DIM + QK_ROPE_DIM
V_HEAD_DIM = 128
Q_LORA_RANK = 512
KV_LORA_RANK = 256
MLA_WIDTH = MLA_HEADS * V_HEAD_DIM
CONV_WIDTH = 512
MEM_HEADS = 4
MEM_HEAD_DIM = 128
MEM_WIDTH = MEM_HEADS * MEM_HEAD_DIM
MIX_WIDTH = MLA_WIDTH + CONV_WIDTH + MEM_WIDTH

LANES = 128
SUBLANES = 8
QK_PAD_DIM = QK_NOPE_DIM + LANES
ONES_ROWS = 2 * SUBLANES
VT_ROWS = V_HEAD_DIM + ONES_ROWS

BF16 = jnp.bfloat16
F32 = jnp.float32

PREP_ROWS = 512
ROPE_COLS = 2048
MIX_ROWS = 512
OUT_ROWS = 512
ATTN_Q = 256
ATTN_K = 512
VMEM_LIMIT = 56 * 1024 * 1024

_NT = (((1,), (1,)), ((), ()))


def _rms(x, g):
    return x * lax.rsqrt(jnp.mean(x * x, axis=-1, keepdims=True) + EPS) * g


def _dot(a, b):
    return jnp.dot(a, b, preferred_element_type=F32)


def _dot_nt(a, b):
    return lax.dot_general(a, b, _NT, preferred_element_type=F32)


def _rope_tables_kernel(pos_ref, invf_ref, cos_t_ref, sin_t_ref):
    ang = invf_ref[...] * pos_ref[...].astype(F32)
    cos_t_ref[...] = jnp.cos(ang)
    sin_t_ref[...] = jnp.sin(ang)


def _rope_tables(pos_row, invf_col):
    tokens = pos_row.shape[1]
    cols = ROPE_COLS
    half = QK_ROPE_DIM // 2
    table = jax.ShapeDtypeStruct((half, tokens), F32)
    return pl.pallas_call(
        _rope_tables_kernel,
        grid=(tokens // cols,),
        in_specs=[pl.BlockSpec((1, cols), lambda i: (0, i)), pl.BlockSpec(invf_col.shape, lambda i: (0, 0))],
        out_specs=[pl.BlockSpec((half, cols), lambda i: (0, i)), pl.BlockSpec((half, cols), lambda i: (0, i))],
        out_shape=[table, table],
        compiler_params=pltpu.CompilerParams(dimension_semantics=("parallel",)),
        name="rope_tables",
    )(pos_row, invf_col)


def _mla_prep_kernel(x_ref, cos_t_ref, sin_t_ref, gpre_ref, w_lat_ref, w_kpe_t_ref, gq_ref, w_uqt_ref, gkv_ref,
                     w_k_ref, w_vt_ref, qt_ref, k_ref, vt_ref):
    h = _rms(x_ref[...], gpre_ref[...]).astype(BF16)
    z = _dot(h, w_lat_ref[...])
    qn = _rms(z[:, :Q_LORA_RANK], gq_ref[...]).astype(BF16)
    kvn = _rms(z[:, Q_LORA_RANK:], gkv_ref[...]).astype(BF16)

    half = QK_ROPE_DIM // 2
    cos_t = cos_t_ref[...]
    sin_t = sin_t_ref[...]

    def rope_t(t):
        x1, x2 = t[:half], t[half:]
        return x1 * cos_t - x2 * sin_t, x2 * cos_t + x1 * sin_t

    rows = z.shape[0]
    k1, k2 = rope_t(_dot_nt(w_kpe_t_ref[...], h))
    kpe_r = jnp.concatenate([k1, k2, jnp.zeros((LANES - QK_ROPE_DIM, rows), F32)], axis=0).T.astype(BF16)

    q_scale = (QK_HEAD_DIM ** -0.5) * math.log2(math.e)
    q_t = _dot_nt(w_uqt_ref[...], qn) * q_scale
    kn = _dot(kvn, w_k_ref[...])
    vt = _dot_nt(w_vt_ref[...], kvn)
    for hd in range(MLA_HEADS):
        lo, hi = hd * LANES, (hd + 1) * LANES
        r_lo = MLA_WIDTH + hd * QK_ROPE_DIM
        q1, q2 = rope_t(q_t[r_lo:r_lo + QK_ROPE_DIM])
        qt_ref[0, hd, :QK_NOPE_DIM, :] = q_t[lo:hi].astype(BF16)
        qt_ref[0, hd, QK_NOPE_DIM:QK_NOPE_DIM + half, :] = q1.astype(BF16)
        qt_ref[0, hd, QK_NOPE_DIM + half:QK_HEAD_DIM, :] = q2.astype(BF16)
        qt_ref[0, hd, QK_HEAD_DIM:, :] = jnp.zeros((QK_PAD_DIM - QK_HEAD_DIM, rows), BF16)
        k_ref[0, hd, :, :QK_NOPE_DIM] = kn[:, lo:hi].astype(BF16)
        k_ref[0, hd, :, QK_NOPE_DIM:] = kpe_r
        vt_ref[0, hd, :V_HEAD_DIM, :] = vt[lo:hi, :].astype(BF16)
        vt_ref[0, hd, V_HEAD_DIM:, :] = jnp.ones((ONES_ROWS, rows), BF16)


def _mla_prep(x2, rope, gpre, w_lat, w_kpe_t, gq, w_uq_t, gkv, w_k, w_vt, batch, seq):
    rows = PREP_ROWS
    per_seq = seq // rows
    half = QK_ROPE_DIM // 2
    const = lambda i: (0, 0)
    head_rows = lambda i: (i // per_seq, 0, i % per_seq, 0)
    head_cols = lambda i: (i // per_seq, 0, 0, i % per_seq)
    return pl.pallas_call(
        _mla_prep_kernel,
        grid=(batch * seq // rows,),
        in_specs=[
            pl.BlockSpec((rows, D_MODEL), lambda i: (i, 0)),
            pl.BlockSpec((half, rows), lambda i: (0, i)),
            pl.BlockSpec((half, rows), lambda i: (0, i)),
            pl.BlockSpec(gpre.shape, const),
            pl.BlockSpec(w_lat.shape, const),
            pl.BlockSpec(w_kpe_t.shape, const),
            pl.BlockSpec(gq.shape, const),
            pl.BlockSpec(w_uq_t.shape, const),
            pl.BlockSpec(gkv.shape, const),
            pl.BlockSpec(w_k.shape, const),
            pl.BlockSpec(w_vt.shape, const),
        ],
        out_specs=[
            pl.BlockSpec((1, MLA_HEADS, QK_PAD_DIM, rows), head_cols),
            pl.BlockSpec((1, MLA_HEADS, rows, QK_PAD_DIM), head_rows),
            pl.BlockSpec((1, MLA_HEADS, VT_ROWS, rows), head_cols),
        ],
        out_shape=[
            jax.ShapeDtypeStruct((batch, MLA_HEADS, QK_PAD_DIM, seq), BF16),
            jax.ShapeDtypeStruct((batch, MLA_HEADS, seq, QK_PAD_DIM), BF16),
            jax.ShapeDtypeStruct((batch, MLA_HEADS, VT_ROWS, seq), BF16),
        ],
        compiler_params=pltpu.CompilerParams(dimension_semantics=("parallel",), vmem_limit_bytes=VMEM_LIMIT),
        name="mla_prep",
    )(x2, *rope, gpre, w_lat, w_kpe_t, gq, w_uq_t, gkv, w_k, w_vt)


def _mla_attn_kernel(qt_ref, k_ref, vt_ref, o_ref, s_buf, acc_buf, *, seq):
    n_blocks = seq // ATTN_Q
    n_chunks = seq // ATTN_K

    def q_block(j):
        return qt_ref[0, 0, :, pl.ds(pl.multiple_of(j * ATTN_Q, ATTN_Q), ATTN_Q)]

    def weighted_values(c, m, acc):
        p = jnp.exp2((s_buf[c * ATTN_K:(c + 1) * ATTN_K, :] - m).astype(BF16))
        part = _dot(vt_ref[0, 0, :, c * ATTN_K:(c + 1) * ATTN_K], p)
        return part if acc is None else acc + part

    def write_scores(c, j):
        s = _dot(k_ref[0, 0, c * ATTN_K:(c + 1) * ATTN_K, :], q_block(j))
        s_buf[c * ATTN_K:(c + 1) * ATTN_K, :] = s
        return jnp.max(s, axis=0, keepdims=True)

    def emit(j):
        acc = acc_buf[...]
        o_t = acc[:V_HEAD_DIM] * (1.0 / acc[V_HEAD_DIM:V_HEAD_DIM + 1])
        o_ref[0, pl.ds(pl.multiple_of(j * ATTN_Q, ATTN_Q), ATTN_Q), :] = o_t.T.astype(o_ref.dtype)

    def col_max(parts):
        return functools.reduce(jnp.maximum, parts)

    def step(j, m_prev):
        emit(jnp.maximum(j - 2, 0))
        acc, maxes = None, []
        for c in range(n_chunks):
            acc = weighted_values(c, m_prev, acc)
            maxes.append(write_scores(c, j))
        acc_buf[...] = acc
        return col_max(maxes)

    m0 = col_max([write_scores(c, 0) for c in range(n_chunks)])
    acc_buf[...] = jnp.ones(acc_buf.shape, F32)
    m = lax.fori_loop(1, n_blocks, step, m0)
    emit(n_blocks - 2)
    acc = None
    for c in range(n_chunks):
        acc = weighted_values(c, m, acc)
    acc_buf[...] = acc
    emit(n_blocks - 1)


def _mla_attn(q_t, k, vt):
    batch, heads, seq, _ = k.shape
    return pl.pallas_call(
        functools.partial(_mla_attn_kernel, seq=seq),
        grid=(batch, heads),
        in_specs=[
            pl.BlockSpec((1, 1, QK_PAD_DIM, seq), lambda b, h: (b, h, 0, 0)),
            pl.BlockSpec((1, 1, seq, QK_PAD_DIM), lambda b, h: (b, h, 0, 0)),
            pl.BlockSpec((1, 1, VT_ROWS, seq), lambda b, h: (b, h, 0, 0)),
        ],
        out_specs=pl.BlockSpec((1, seq, V_HEAD_DIM), lambda b, h: (b, 0, h)),
        out_shape=jax.ShapeDtypeStruct((batch, seq, MLA_WIDTH), BF16),
        scratch_shapes=[pltpu.VMEM((seq, ATTN_Q), F32), pltpu.VMEM((VT_ROWS, ATTN_Q), F32)],
        compiler_params=pltpu.CompilerParams(
            dimension_semantics=("parallel", "parallel"), vmem_limit_bytes=VMEM_LIMIT),
        name="mla_attn",
    )(q_t, k, vt)


def _mem_kv_kernel(mem_ref, g_ref, w_mk_ref, w_mv_ref, mk_ref, mv_ref):
    mem_n = _rms(mem_ref[0], g_ref[0]).astype(BF16)
    mk_ref[0, 0] = _dot(mem_n, w_mk_ref[0]).astype(BF16)
    mv_ref[0, 0] = _dot(mem_n, w_mv_ref[0]).astype(BF16)


def _mem_kv(mem, g, w_mk, w_mv):
    batch = mem.shape[0]
    out = jax.ShapeDtypeStruct((DEPTH, batch, MEM_TOKENS, MEM_WIDTH), BF16)
    return pl.pallas_call(
        _mem_kv_kernel,
        grid=(DEPTH, batch),
        in_specs=[
            pl.BlockSpec((1, MEM_TOKENS, D_MODEL), lambda l, b: (b, 0, 0)),
            pl.BlockSpec((1, 1, D_MODEL), lambda l, b: (l, 0, 0)),
            pl.BlockSpec((1, D_MODEL, MEM_WIDTH), lambda l, b: (l, 0, 0)),
            pl.BlockSpec((1, D_MODEL, MEM_WIDTH), lambda l, b: (l, 0, 0)),
        ],
        out_specs=[
            pl.BlockSpec((1, 1, MEM_TOKENS, MEM_WIDTH), lambda l, b: (l, b, 0, 0)),
            pl.BlockSpec((1, 1, MEM_TOKENS, MEM_WIDTH), lambda l, b: (l, b, 0, 0)),
        ],
        out_shape=[out, out],
        compiler_params=pltpu.CompilerParams(dimension_semantics=("parallel", "parallel")),
        name="mem_kv",
    )(mem, g, w_mk, w_mv)


def _mix_kernel(x_ref, xp_ref, xn_ref, a_ref, mk_ref, mv_ref, gpre_ref, w_conv_ref, w_other_ref, cw_ref, y_ref,
                *, seq):
    rows = x_ref.shape[0]
    i = pl.program_id(0)
    keep_prev = ((i * rows) % seq != 0).astype(F32)
    keep_next = (((i + 1) * rows) % seq != 0).astype(F32)
    x_ext = jnp.concatenate([xp_ref[...] * keep_prev, x_ref[...], xn_ref[...] * keep_next], axis=0)
    h_ext = _rms(x_ext, gpre_ref[...]).astype(BF16)

    zc = _dot(h_ext, w_conv_ref[...])
    u = zc[:, :CONV_WIDTH] * zc[:, CONV_WIDTH:]
    ext = rows + 2 * SUBLANES
    body = slice(SUBLANES, SUBLANES + rows)
    conv = (pltpu.roll(u, 1, 0)[body] * cw_ref[0:1, :] + u[body] * cw_ref[1:2, :]
            + pltpu.roll(u, ext - 1, 0)[body] * cw_ref[2:3, :])

    zo = _dot(h_ext[body], w_other_ref[...])
    c_out = zo[:, :CONV_WIDTH] * conv
    gate = zo[:, CONV_WIDTH + MEM_WIDTH:]
    act = gate * (1.0 / (1.0 + jnp.exp(-gate)))

    y_ref[:, :MLA_WIDTH] = (a_ref[...].astype(F32) * act[:, :MLA_WIDTH]).astype(BF16)
    y_ref[:, MLA_WIDTH:MLA_WIDTH + CONV_WIDTH] = (c_out * act[:, MLA_WIDTH:MLA_WIDTH + CONV_WIDTH]).astype(BF16)

    mem_scale = MEM_HEAD_DIM ** -0.5
    for hd in range(MEM_HEADS):
        lo, hi = hd * MEM_HEAD_DIM, (hd + 1) * MEM_HEAD_DIM
        qh = (zo[:, CONV_WIDTH + lo:CONV_WIDTH + hi] * mem_scale).astype(BF16)
        s = _dot_nt(qh, mk_ref[0, :, lo:hi])
        p = jnp.exp(s - jnp.max(s, axis=-1, keepdims=True))
        o = _dot(p.astype(BF16), mv_ref[0, :, lo:hi]) * (1.0 / jnp.sum(p, axis=-1, keepdims=True))
        col = MLA_WIDTH + CONV_WIDTH + lo
        y_ref[:, col:col + MEM_HEAD_DIM] = (o * act[:, col:col + MEM_HEAD_DIM]).astype(BF16)


def _mix(x2, a2, mk, mv, layer, gpre, w_conv, w_other, cw, batch, seq):
    rows = MIX_ROWS
    tokens = batch * seq
    per_seq = seq // rows
    tiles = rows // SUBLANES
    last_tile = tokens // SUBLANES - 1
    const = lambda i: (0, 0)
    single = pl.Buffered(1)
    return pl.pallas_call(
        functools.partial(_mix_kernel, seq=seq),
        grid=(tokens // rows,),
        in_specs=[
            pl.BlockSpec((rows, D_MODEL), lambda i: (i, 0)),
            pl.BlockSpec((SUBLANES, D_MODEL), lambda i: (jnp.maximum(i * tiles - 1, 0), 0)),
            pl.BlockSpec((SUBLANES, D_MODEL), lambda i: (jnp.minimum((i + 1) * tiles, last_tile), 0)),
            pl.BlockSpec((rows, MLA_WIDTH), lambda i: (i, 0)),
            pl.BlockSpec((None, 1, MEM_TOKENS, MEM_WIDTH), lambda i: (layer, i // per_seq, 0, 0)),
            pl.BlockSpec((None, 1, MEM_TOKENS, MEM_WIDTH), lambda i: (layer, i // per_seq, 0, 0)),
            pl.BlockSpec(gpre.shape, const),
            pl.BlockSpec(w_conv.shape, const, pipeline_mode=single),
            pl.BlockSpec(w_other.shape, const, pipeline_mode=single),
            pl.BlockSpec(cw.shape, const),
        ],
        out_specs=pl.BlockSpec((rows, MIX_WIDTH), lambda i: (i, 0)),
        out_shape=jax.ShapeDtypeStruct((tokens, MIX_WIDTH), BF16),
        compiler_params=pltpu.CompilerParams(dimension_semantics=("parallel",), vmem_limit_bytes=VMEM_LIMIT),
        name="mix",
    )(x2, x2, x2, a2, mk, mv, gpre, w_conv, w_other, cw)


def _out_proj_kernel(y_ref, w_o_ref, x_ref, g_ref, o_ref):
    o = _dot(y_ref[...], w_o_ref[...])
    o_ref[...] = x_ref[...] + _rms(o, g_ref[...])


def _out_proj(y2, w_o, x2, g):
    rows = OUT_ROWS
    tokens = x2.shape[0]
    const = lambda i: (0, 0)
    return pl.pallas_call(
        _out_proj_kernel,
        grid=(tokens // rows,),
        in_specs=[
            pl.BlockSpec((rows, MIX_WIDTH), lambda i: (i, 0)),
            pl.BlockSpec(w_o.shape, const, pipeline_mode=pl.Buffered(1)),
            pl.BlockSpec((rows, D_MODEL), lambda i: (i, 0)),
            pl.BlockSpec(g.shape, const),
        ],
        out_specs=pl.BlockSpec((rows, D_MODEL), lambda i: (i, 0)),
        out_shape=jax.ShapeDtypeStruct(x2.shape, F32),
        compiler_params=pltpu.CompilerParams(dimension_semantics=("parallel",), vmem_limit_bytes=VMEM_LIMIT),
        name="out_proj",
    )(y2, w_o, x2, g)


def _rope_inv_freq_col():
    inv_freq = 1.0 / (ROPE_THETA ** (jnp.arange(0, QK_ROPE_DIM, 2, dtype=F32) / QK_ROPE_DIM))
    return inv_freq[:, None]


def kernel(x, mem, positions, pre_norm_g, w_in, q_norm_g, w_uq, kv_norm_g, w_ukv, conv_w, mem_norm_g, w_mk, w_mv,
           w_o, post_norm_g):
    batch, seq, _ = x.shape
    tokens = batch * seq
    assert seq % PREP_ROWS == 0 and seq % MIX_ROWS == 0 and seq % ATTN_Q == 0 and seq % ATTN_K == 0
    assert tokens % OUT_ROWS == 0

    c0 = Q_LORA_RANK + KV_LORA_RANK
    c1 = c0 + QK_ROPE_DIM
    c_gb, c_gc, c_xin, c_qm = c1, c1 + CONV_WIDTH, c1 + 2 * CONV_WIDTH, c1 + 3 * CONV_WIDTH
    c_gate = c_qm + MEM_WIDTH
    w_in_b = w_in.astype(BF16)
    w_lat = w_in_b[:, :, :c0]
    w_kpe_t = jnp.swapaxes(w_in_b[:, :, c0:c1], 1, 2)
    w_conv = w_in_b[:, :, c_gc:c_qm]
    w_other = jnp.concatenate([w_in_b[:, :, c_gb:c_gc], w_in_b[:, :, c_qm:]], axis=-1)
    del c_gate

    w_uq_h = w_uq.astype(BF16).reshape(DEPTH, Q_LORA_RANK, MLA_HEADS, QK_HEAD_DIM)
    w_uq_nope = w_uq_h[..., :QK_NOPE_DIM].reshape(DEPTH, Q_LORA_RANK, MLA_WIDTH)
    w_uq_rope = w_uq_h[..., QK_NOPE_DIM:].reshape(DEPTH, Q_LORA_RANK, MLA_HEADS * QK_ROPE_DIM)
    w_uq_t = jnp.swapaxes(jnp.concatenate([w_uq_nope, w_uq_rope], axis=-1), 1, 2)

    w_ukv_h = w_ukv.astype(BF16).reshape(DEPTH, KV_LORA_RANK, MLA_HEADS, QK_NOPE_DIM + V_HEAD_DIM)
    w_k = w_ukv_h[..., :QK_NOPE_DIM].reshape(DEPTH, KV_LORA_RANK, MLA_WIDTH)
    w_vt = jnp.swapaxes(w_ukv_h[..., QK_NOPE_DIM:].reshape(DEPTH, KV_LORA_RANK, MLA_WIDTH), 1, 2)

    w_o_b = w_o.astype(BF16)
    rope = _rope_tables(positions.reshape(1, tokens), _rope_inv_freq_col())
    mk, mv = _mem_kv(mem, mem_norm_g[:, None, :], w_mk.astype(BF16), w_mv.astype(BF16))

    x2 = x.reshape(tokens, D_MODEL)
    for l in range(DEPTH):
        gpre = pre_norm_g[l][None, :]
        q_t, k, vt = _mla_prep(x2, rope, gpre, w_lat[l], w_kpe_t[l], q_norm_g[l][None, :], w_uq_t[l],
                               kv_norm_g[l][None, :], w_k[l], w_vt[l], batch, seq)
        a = _mla_attn(q_t, k, vt)
        y = _mix(x2, a.reshape(tokens, MLA_WIDTH), mk, mv, l, gpre, w_conv[l], w_other[l], conv_w[l], batch, seq)
        x2 = _out_proj(y, w_o_b[l], x2, post_norm_g[l][None, :])
    return x2.reshape(batch, seq, D_MODEL)
```

```python
import functools
import math

import jax
import jax.numpy as jnp
from jax import lax
from jax.experimental import pallas as pl
from jax.experimental.pallas import tpu as pltpu

D_MODEL = 2048
DEPTH = 4
MEM_TOKENS = 256
EPS = 1e-6
ROPE_THETA = 10000.0
MLA_HEADS = 8
QK_NOPE_DIM = 128
QK_ROPE_DIM = 64
QK_HEAD_DIM = QK_NOPE_DIM + QK_ROPE_DIM
V_HEAD_DIM = 128
Q_LORA_RANK = 512
KV_LORA_RANK = 256
MLA_WIDTH = MLA_HEADS * V_HEAD_DIM
CONV_WIDTH = 512
MEM_HEADS = 4
MEM_HEAD_DIM = 128
MEM_WIDTH = MEM_HEADS * MEM_HEAD_DIM
MIX_WIDTH = MLA_WIDTH + CONV_WIDTH + MEM_WIDTH

LANES = 128
SUBLANES = 8
QK_PAD_DIM = QK_NOPE_DIM + LANES
ONES_ROWS = 2 * SUBLANES
VT_ROWS = V_HEAD_DIM + ONES_ROWS

BF16 = jnp.bfloat16
F32 = jnp.float32

PREP_ROWS = 512
ROPE_COLS = 2048
MIX_ROWS = 512
OUT_ROWS = 512
ATTN_Q = 256
ATTN_K = 512
VMEM_LIMIT = 56 * 1024 * 1024

_NT = (((1,), (1,)), ((), ()))


def _rms(x, g):
    return x * lax.rsqrt(jnp.mean(x * x, axis=-1, keepdims=True) + EPS) * g


def _dot(a, b):
    return jnp.dot(a, b, preferred_element_type=F32)


def _dot_nt(a, b):
    return lax.dot_general(a, b, _NT, preferred_element_type=F32)


def _layer_spec(stacked, layer, **kwargs):
    return pl.BlockSpec((None,) + stacked.shape[1:], lambda i: (layer, 0, 0), **kwargs)


def _rope_tables_kernel(pos_ref, invf_ref, cos_t_ref, sin_t_ref):
    ang = invf_ref[...] * pos_ref[...].astype(F32)
    cos_t_ref[...] = jnp.cos(ang)
    sin_t_ref[...] = jnp.sin(ang)


def _rope_tables(pos_row, invf_col):
    tokens = pos_row.shape[1]
    cols = ROPE_COLS
    half = QK_ROPE_DIM // 2
    table = jax.ShapeDtypeStruct((half, tokens), F32)
    return pl.pallas_call(
        _rope_tables_kernel,
        grid=(tokens // cols,),
        in_specs=[pl.BlockSpec((1, cols), lambda i: (0, i)), pl.BlockSpec(invf_col.shape, lambda i: (0, 0))],
        out_specs=[pl.BlockSpec((half, cols), lambda i: (0, i)), pl.BlockSpec((half, cols), lambda i: (0, i))],
        out_shape=[table, table],
        compiler_params=pltpu.CompilerParams(dimension_semantics=("parallel",)),
        name="rope_tables",
    )(pos_row, invf_col)


def _mla_prep_kernel(x_ref, cos_t_ref, sin_t_ref, gpre_ref, w_lat_ref, w_kpe_t_ref, gq_ref, w_uqt_ref, gkv_ref,
                     w_k_ref, w_vt_ref, qt_ref, k_ref, vt_ref):
    h = _rms(x_ref[...], gpre_ref[...]).astype(BF16)
    z = _dot(h, w_lat_ref[...])
    qn = _rms(z[:, :Q_LORA_RANK], gq_ref[...]).astype(BF16)
    kvn = _rms(z[:, Q_LORA_RANK:], gkv_ref[...]).astype(BF16)

    half = QK_ROPE_DIM // 2
    cos_t = cos_t_ref[...]
    sin_t = sin_t_ref[...]

    def rope_t(t):
        x1, x2 = t[:half], t[half:]
        return x1 * cos_t - x2 * sin_t, x2 * cos_t + x1 * sin_t

    rows = z.shape[0]
    k1, k2 = rope_t(_dot_nt(w_kpe_t_ref[...], h))
    kpe_r = jnp.concatenate([k1, k2, jnp.zeros((LANES - QK_ROPE_DIM, rows), F32)], axis=0).T.astype(BF16)

    q_scale = (QK_HEAD_DIM ** -0.5) * math.log2(math.e)
    q_t = _dot_nt(w_uqt_ref[...], qn) * q_scale
    kn = _dot(kvn, w_k_ref[...])
    vt = _dot_nt(w_vt_ref[...], kvn)
    for hd in range(MLA_HEADS):
        lo, hi = hd * LANES, (hd + 1) * LANES
        r_lo = MLA_WIDTH + hd * QK_ROPE_DIM
        q1, q2 = rope_t(q_t[r_lo:r_lo + QK_ROPE_DIM])
        qt_ref[0, hd, :QK_NOPE_DIM, :] = q_t[lo:hi].astype(BF16)
        qt_ref[0, hd, QK_NOPE_DIM:QK_NOPE_DIM + half, :] = q1.astype(BF16)
        qt_ref[0, hd, QK_NOPE_DIM + half:QK_HEAD_DIM, :] = q2.astype(BF16)
        qt_ref[0, hd, QK_HEAD_DIM:, :] = jnp.zeros((QK_PAD_DIM - QK_HEAD_DIM, rows), BF16)
        k_ref[0, hd, :, :QK_NOPE_DIM] = kn[:, lo:hi].astype(BF16)
        k_ref[0, hd, :, QK_NOPE_DIM:] = kpe_r
        vt_ref[0, hd, :V_HEAD_DIM, :] = vt[lo:hi, :].astype(BF16)
        vt_ref[0, hd, V_HEAD_DIM:, :] = jnp.ones((ONES_ROWS, rows), BF16)


def _mla_prep(x2, rope, layer, gpre, w_lat, w_kpe_t, gq, w_uq_t, gkv, w_k, w_vt, batch, seq):
    rows = PREP_ROWS
    per_seq = seq // rows
    half = QK_ROPE_DIM // 2
    head_rows = lambda i: (i // per_seq, 0, i % per_seq, 0)
    head_cols = lambda i: (i // per_seq, 0, 0, i % per_seq)
    return pl.pallas_call(
        _mla_prep_kernel,
        grid=(batch * seq // rows,),
        in_specs=[
            pl.BlockSpec((rows, D_MODEL), lambda i: (i, 0)),
            pl.BlockSpec((half, rows), lambda i: (0, i)),
            pl.BlockSpec((half, rows), lambda i: (0, i)),
            _layer_spec(gpre, layer),
            _layer_spec(w_lat, layer),
            _layer_spec(w_kpe_t, layer),
            _layer_spec(gq, layer),
            _layer_spec(w_uq_t, layer),
            _layer_spec(gkv, layer),
            _layer_spec(w_k, layer),
            _layer_spec(w_vt, layer),
        ],
        out_specs=[
            pl.BlockSpec((1, MLA_HEADS, QK_PAD_DIM, rows), head_cols),
            pl.BlockSpec((1, MLA_HEADS, rows, QK_PAD_DIM), head_rows),
            pl.BlockSpec((1, MLA_HEADS, VT_ROWS, rows), head_cols),
        ],
        out_shape=[
            jax.ShapeDtypeStruct((batch, MLA_HEADS, QK_PAD_DIM, seq), BF16),
            jax.ShapeDtypeStruct((batch, MLA_HEADS, seq, QK_PAD_DIM), BF16),
            jax.ShapeDtypeStruct((batch, MLA_HEADS, VT_ROWS, seq), BF16),
        ],
        compiler_params=pltpu.CompilerParams(dimension_semantics=("parallel",), vmem_limit_bytes=VMEM_LIMIT),
        name="mla_prep",
    )(x2, *rope, gpre, w_lat, w_kpe_t, gq, w_uq_t, gkv, w_k, w_vt)


def _mla_attn_kernel(qt_ref, k_ref, vt_ref, o_ref, s_buf, *, seq):
    n_blocks = seq // ATTN_Q
    n_chunks = seq // ATTN_K

    def q_block(j):
        return qt_ref[0, 0, :, pl.ds(pl.multiple_of(j * ATTN_Q, ATTN_Q), ATTN_Q)]

    def read_probs(c, m):
        return jnp.exp2((s_buf[c * ATTN_K:(c + 1) * ATTN_K, :] - m).astype(BF16))

    def write_scores(c, q_t):
        s = _dot(k_ref[0, 0, c * ATTN_K:(c + 1) * ATTN_K, :], q_t)
        s_buf[c * ATTN_K:(c + 1) * ATTN_K, :] = s
        return jnp.max(s, axis=0, keepdims=True)

    def weighted_values(probs):
        return _dot(vt_ref[0, 0], jnp.concatenate(probs, axis=0))

    def emit(j, acc):
        o_t = acc[:V_HEAD_DIM] * (1.0 / acc[V_HEAD_DIM:V_HEAD_DIM + 1])
        o_ref[0, pl.ds(pl.multiple_of(j * ATTN_Q, ATTN_Q), ATTN_Q), :] = o_t.T.astype(o_ref.dtype)

    def col_max(parts):
        return functools.reduce(jnp.maximum, parts)

    def step(j, carry):
        m_prev, acc_prev = carry
        emit(jnp.maximum(j - 2, 0), acc_prev)
        q_t = q_block(j)
        probs, maxes = [], []
        for c in range(n_chunks):
            probs.append(read_probs(c, m_prev))
            maxes.append(write_scores(c, q_t))
        return col_max(maxes), weighted_values(probs)

    q0 = q_block(0)
    m0 = col_max([write_scores(c, q0) for c in range(n_chunks)])
    m, acc = lax.fori_loop(1, n_blocks, step, (m0, jnp.ones((VT_ROWS, ATTN_Q), F32)))
    emit(n_blocks - 2, acc)
    emit(n_blocks - 1, weighted_values([read_probs(c, m) for c in range(n_chunks)]))


def _mla_attn(q_t, k, vt):
    batch, heads, seq, _ = k.shape
    return pl.pallas_call(
        functools.partial(_mla_attn_kernel, seq=seq),
        grid=(batch, heads),
        in_specs=[
            pl.BlockSpec((1, 1, QK_PAD_DIM, seq), lambda b, h: (b, h, 0, 0)),
            pl.BlockSpec((1, 1, seq, QK_PAD_DIM), lambda b, h: (b, h, 0, 0)),
            pl.BlockSpec((1, 1, VT_ROWS, seq), lambda b, h: (b, h, 0, 0)),
        ],
        out_specs=pl.BlockSpec((1, seq, V_HEAD_DIM), lambda b, h: (b, 0, h)),
        out_shape=jax.ShapeDtypeStruct((batch, seq, MLA_WIDTH), BF16),
        scratch_shapes=[pltpu.VMEM((seq, ATTN_Q), F32)],
        compiler_params=pltpu.CompilerParams(
            dimension_semantics=("parallel", "parallel"), vmem_limit_bytes=VMEM_LIMIT),
        name="mla_attn",
    )(q_t, k, vt)


def _mem_kv_kernel(mem_ref, g_ref, w_mk_ref, w_mv_ref, mk_ref, mv_ref):
    mem_n = _rms(mem_ref[0], g_ref[0]).astype(BF16)
    mk_ref[0, 0] = _dot(mem_n, w_mk_ref[0]).astype(BF16)
    mv_ref[0, 0] = _dot(mem_n, w_mv_ref[0]).astype(BF16)


def _mem_kv(mem, g, w_mk, w_mv):
    batch = mem.shape[0]
    out = jax.ShapeDtypeStruct((DEPTH, batch, MEM_TOKENS, MEM_WIDTH), BF16)
    return pl.pallas_call(
        _mem_kv_kernel,
        grid=(DEPTH, batch),
        in_specs=[
            pl.BlockSpec((1, MEM_TOKENS, D_MODEL), lambda l, b: (b, 0, 0)),
            pl.BlockSpec((1, 1, D_MODEL), lambda l, b: (l, 0, 0)),
            pl.BlockSpec((1, D_MODEL, MEM_WIDTH), lambda l, b: (l, 0, 0)),
            pl.BlockSpec((1, D_MODEL, MEM_WIDTH), lambda l, b: (l, 0, 0)),
        ],
        out_specs=[
            pl.BlockSpec((1, 1, MEM_TOKENS, MEM_WIDTH), lambda l, b: (l, b, 0, 0)),
            pl.BlockSpec((1, 1, MEM_TOKENS, MEM_WIDTH), lambda l, b: (l, b, 0, 0)),
        ],
        out_shape=[out, out],
        compiler_params=pltpu.CompilerParams(dimension_semantics=("parallel", "parallel")),
        name="mem_kv",
    )(mem, g, w_mk, w_mv)


def _mix_kernel(x_ref, xp_ref, xn_ref, a_ref, mk_ref, mv_ref, gpre_ref, w_conv_ref, w_other_ref, cw_ref, y_ref,
                *, seq):
    rows = x_ref.shape[0]
    i = pl.program_id(0)
    keep_prev = ((i * rows) % seq != 0).astype(F32)
    keep_next = (((i + 1) * rows) % seq != 0).astype(F32)
    x_ext = jnp.concatenate([xp_ref[...] * keep_prev, x_ref[...], xn_ref[...] * keep_next], axis=0)
    h_ext = _rms(x_ext, gpre_ref[...]).astype(BF16)

    zc = _dot(h_ext, w_conv_ref[...])
    u = zc[:, :CONV_WIDTH] * zc[:, CONV_WIDTH:]
    ext = rows + 2 * SUBLANES
    body = slice(SUBLANES, SUBLANES + rows)
    conv = (pltpu.roll(u, 1, 0)[body] * cw_ref[0:1, :] + u[body] * cw_ref[1:2, :]
            + pltpu.roll(u, ext - 1, 0)[body] * cw_ref[2:3, :])

    zo = _dot(h_ext[body], w_other_ref[...])
    c_out = zo[:, :CONV_WIDTH] * conv
    gate = zo[:, CONV_WIDTH + MEM_WIDTH:]
    act = gate * (1.0 / (1.0 + jnp.exp(-gate)))

    y_ref[:, :MLA_WIDTH] = (a_ref[...].astype(F32) * act[:, :MLA_WIDTH]).astype(BF16)
    y_ref[:, MLA_WIDTH:MLA_WIDTH + CONV_WIDTH] = (c_out * act[:, MLA_WIDTH:MLA_WIDTH + CONV_WIDTH]).astype(BF16)

    mem_scale = MEM_HEAD_DIM ** -0.5
    for hd in range(MEM_HEADS):
        lo, hi = hd * MEM_HEAD_DIM, (hd + 1) * MEM_HEAD_DIM
        qh = (zo[:, CONV_WIDTH + lo:CONV_WIDTH + hi] * mem_scale).astype(BF16)
        s = _dot_nt(qh, mk_ref[0, :, lo:hi])
        p = jnp.exp(s - jnp.max(s, axis=-1, keepdims=True))
        o = _dot(p.astype(BF16), mv_ref[0, :, lo:hi]) * (1.0 / jnp.sum(p, axis=-1, keepdims=True))
        col = MLA_WIDTH + CONV_WIDTH + lo
        y_ref[:, col:col + MEM_HEAD_DIM] = (o * act[:, col:col + MEM_HEAD_DIM]).astype(BF16)


def _mix(x2, a2, mk, mv, layer, gpre, w_conv, w_other, cw, batch, seq):
    rows = MIX_ROWS
    tokens = batch * seq
    per_seq = seq // rows
    tiles = rows // SUBLANES
    last_tile = tokens // SUBLANES - 1
    single = pl.Buffered(1)
    return pl.pallas_call(
        functools.partial(_mix_kernel, seq=seq),
        grid=(tokens // rows,),
        in_specs=[
            pl.BlockSpec((rows, D_MODEL), lambda i: (i, 0)),
            pl.BlockSpec((SUBLANES, D_MODEL), lambda i: (jnp.maximum(i * tiles - 1, 0), 0)),
            pl.BlockSpec((SUBLANES, D_MODEL), lambda i: (jnp.minimum((i + 1) * tiles, last_tile), 0)),
            pl.BlockSpec((rows, MLA_WIDTH), lambda i: (i, 0)),
            pl.BlockSpec((None, 1, MEM_TOKENS, MEM_WIDTH), lambda i: (layer, i // per_seq, 0, 0)),
            pl.BlockSpec((None, 1, MEM_TOKENS, MEM_WIDTH), lambda i: (layer, i // per_seq, 0, 0)),
            _layer_spec(gpre, layer),
            _layer_spec(w_conv, layer, pipeline_mode=single),
            _layer_spec(w_other, layer, pipeline_mode=single),
            _layer_spec(cw, layer),
        ],
        out_specs=pl.BlockSpec((rows, MIX_WIDTH), lambda i: (i, 0)),
        out_shape=jax.ShapeDtypeStruct((tokens, MIX_WIDTH), BF16),
        compiler_params=pltpu.CompilerParams(dimension_semantics=("parallel",), vmem_limit_bytes=VMEM_LIMIT),
        name="mix",
    )(x2, x2, x2, a2, mk, mv, gpre, w_conv, w_other, cw)


def _out_proj_kernel(y_ref, w_o_ref, x_ref, g_ref, o_ref):
    o = _dot(y_ref[...], w_o_ref[...])
    o_ref[...] = x_ref[...] + _rms(o, g_ref[...])


def _out_proj(y2, w_o, x2, g, layer):
    rows = OUT_ROWS
    tokens = x2.shape[0]
    return pl.pallas_call(
        _out_proj_kernel,
        grid=(tokens // rows,),
        in_specs=[
            pl.BlockSpec((rows, MIX_WIDTH), lambda i: (i, 0)),
            _layer_spec(w_o, layer, pipeline_mode=pl.Buffered(1)),
            pl.BlockSpec((rows, D_MODEL), lambda i: (i, 0)),
            _layer_spec(g, layer),
        ],
        out_specs=pl.BlockSpec((rows, D_MODEL), lambda i: (i, 0)),
        out_shape=jax.ShapeDtypeStruct(x2.shape, F32),
        compiler_params=pltpu.CompilerParams(dimension_semantics=("parallel",), vmem_limit_bytes=VMEM_LIMIT),
        name="out_proj",
    )(y2, w_o, x2, g)


def _rope_inv_freq_col():
    inv_freq = 1.0 / (ROPE_THETA ** (jnp.arange(0, QK_ROPE_DIM, 2, dtype=F32) / QK_ROPE_DIM))
    return inv_freq[:, None]


def kernel(x, mem, positions, pre_norm_g, w_in, q_norm_g, w_uq, kv_norm_g, w_ukv, conv_w, mem_norm_g, w_mk, w_mv,
           w_o, post_norm_g):
    batch, seq, _ = x.shape
    tokens = batch * seq
    assert seq % PREP_ROWS == 0 and seq % MIX_ROWS == 0 and seq % ATTN_Q == 0 and seq % ATTN_K == 0
    assert tokens % OUT_ROWS == 0

    c0 = Q_LORA_RANK + KV_LORA_RANK
    c1 = c0 + QK_ROPE_DIM
    c_gb, c_gc, c_qm = c1, c1 + CONV_WIDTH, c1 + 3 * CONV_WIDTH
    w_lat = w_in[:, :, :c0].astype(BF16)
    w_kpe_t = jnp.swapaxes(w_in[:, :, c0:c1], 1, 2).astype(BF16)
    w_conv = w_in[:, :, c_gc:c_qm].astype(BF16)
    w_other = jnp.concatenate([w_in[:, :, c_gb:c_gc], w_in[:, :, c_qm:]], axis=-1).astype(BF16)

    w_uq_h = w_uq.reshape(DEPTH, Q_LORA_RANK, MLA_HEADS, QK_HEAD_DIM)
    w_uq_nope = w_uq_h[..., :QK_NOPE_DIM].reshape(DEPTH, Q_LORA_RANK, MLA_WIDTH)
    w_uq_rope = w_uq_h[..., QK_NOPE_DIM:].reshape(DEPTH, Q_LORA_RANK, MLA_HEADS * QK_ROPE_DIM)
    w_uq_t = jnp.swapaxes(jnp.concatenate([w_uq_nope, w_uq_rope], axis=-1), 1, 2).astype(BF16)

    w_ukv_h = w_ukv.reshape(DEPTH, KV_LORA_RANK, MLA_HEADS, QK_NOPE_DIM + V_HEAD_DIM)
    w_k = w_ukv_h[..., :QK_NOPE_DIM].reshape(DEPTH, KV_LORA_RANK, MLA_WIDTH).astype(BF16)
    w_vt = jnp.swapaxes(w_ukv_h[..., QK_NOPE_DIM:].reshape(DEPTH, KV_LORA_RANK, MLA_WIDTH), 1, 2).astype(BF16)

    w_o_b = w_o.astype(BF16)
    gpre, gq, gkv, gpost = (g[:, None, :] for g in (pre_norm_g, q_norm_g, kv_norm_g, post_norm_g))
    rope = _rope_tables(positions.reshape(1, tokens), _rope_inv_freq_col())
    mk, mv = _mem_kv(mem, mem_norm_g[:, None, :], w_mk.astype(BF16), w_mv.astype(BF16))

    x2 = x.reshape(tokens, D_MODEL)
    for l in range(DEPTH):
        q_t, k, vt = _mla_prep(x2, rope, l, gpre, w_lat, w_kpe_t, gq, w_uq_t, gkv, w_k, w_vt, batch, seq)
        a = _mla_attn(q_t, k, vt)
        y = _mix(x2, a.reshape(tokens, MLA_WIDTH), mk, mv, l, gpre, w_conv, w_other, conv_w, batch, seq)
        x2 = _out_proj(y, w_o_b, x2, gpost, l)
    return x2.reshape(batch, seq, D_MODEL)
```

```python
import functools
import math

import jax
import jax.numpy as jnp
from jax import lax
from jax.experimental import pallas as pl
from jax.experimental.pallas import tpu as pltpu

D_MODEL = 2048
DEPTH = 4
MEM_TOKENS = 256
EPS = 1e-6
ROPE_THETA = 10000.0
MLA_HEADS = 8
QK_NOPE_DIM = 128
QK_ROPE_DIM = 64
QK_HEAD_DIM = QK_NOPE_DIM + QK_ROPE_DIM
V_HEAD_DIM = 128
Q_LORA_RANK = 512
KV_LORA_RANK = 256
MLA_WIDTH = MLA_HEADS * V_HEAD_DIM
CONV_WIDTH = 512
MEM_HEADS = 4
MEM_HEAD_DIM = 128
MEM_WIDTH = MEM_HEADS * MEM_HEAD_DIM
MIX_WIDTH = MLA_WIDTH + CONV_WIDTH + MEM_WIDTH

LANES = 128
SUBLANES = 8
QK_PAD_DIM = QK_NOPE_DIM + LANES
ONES_ROWS = 2 * SUBLANES
VT_ROWS = V_HEAD_DIM + ONES_ROWS

BF16 = jnp.bfloat16
F32 = jnp.float32

PREP_ROWS = 512
ROPE_COLS = 2048
MIX_ROWS = 512
GATE_COLS = 512
OUT_ROWS = 512
ATTN_Q = 256
ATTN_K = 512
VMEM_LIMIT = 56 * 1024 * 1024

_NT = (((1,), (1,)), ((), ()))


def _rms(x, g):
    return x * lax.rsqrt(jnp.mean(x * x, axis=-1, keepdims=True) + EPS) * g


def _dot(a, b):
    return jnp.dot(a, b, preferred_element_type=F32)


def _dot_nt(a, b):
    return lax.dot_general(a, b, _NT, preferred_element_type=F32)


def _layer_spec(stacked, layer, **kwargs):
    return pl.BlockSpec((None,) + stacked.shape[1:], lambda i: (layer, 0, 0), **kwargs)


def _rope_tables_kernel(pos_ref, invf_ref, cos_t_ref, sin_t_ref):
    ang = invf_ref[...] * pos_ref[...].astype(F32)
    cos_t_ref[...] = jnp.cos(ang)
    sin_t_ref[...] = jnp.sin(ang)


def _rope_tables(pos_row, invf_col):
    tokens = pos_row.shape[1]
    cols = ROPE_COLS
    half = QK_ROPE_DIM // 2
    table = jax.ShapeDtypeStruct((half, tokens), F32)
    return pl.pallas_call(
        _rope_tables_kernel,
        grid=(tokens // cols,),
        in_specs=[pl.BlockSpec((1, cols), lambda i: (0, i)), pl.BlockSpec(invf_col.shape, lambda i: (0, 0))],
        out_specs=[pl.BlockSpec((half, cols), lambda i: (0, i)), pl.BlockSpec((half, cols), lambda i: (0, i))],
        out_shape=[table, table],
        compiler_params=pltpu.CompilerParams(dimension_semantics=("parallel",)),
        name="rope_tables",
    )(pos_row, invf_col)


def _mla_prep_kernel(x_ref, cos_t_ref, sin_t_ref, gpre_ref, w_lat_ref, gq_ref, w_uqt_ref, gkv_ref,
                     w_k_ref, w_vt_ref, qt_ref, k_ref, vt_ref):
    h = _rms(x_ref[...], gpre_ref[...]).astype(BF16)
    z = _dot(h, w_lat_ref[...])
    qn = _rms(z[:, :Q_LORA_RANK], gq_ref[...]).astype(BF16)
    kvn = _rms(z[:, Q_LORA_RANK:Q_LORA_RANK + KV_LORA_RANK], gkv_ref[...]).astype(BF16)
    kpe_t = z[:, Q_LORA_RANK + KV_LORA_RANK:].T

    half = QK_ROPE_DIM // 2
    cos_t = cos_t_ref[...]
    sin_t = sin_t_ref[...]

    def rope_t(t):
        x1, x2 = t[:half], t[half:]
        return x1 * cos_t - x2 * sin_t, x2 * cos_t + x1 * sin_t

    rows = z.shape[0]
    k1, k2 = rope_t(kpe_t[:QK_ROPE_DIM])
    kpe_r = jnp.concatenate([k1, k2, jnp.zeros((LANES - QK_ROPE_DIM, rows), F32)], axis=0).T.astype(BF16)

    q_scale = (QK_HEAD_DIM ** -0.5) * math.log2(math.e)
    q_t = _dot_nt(w_uqt_ref[...], qn) * q_scale
    kn = _dot(kvn, w_k_ref[...])
    vt = _dot_nt(w_vt_ref[...], kvn)
    for hd in range(MLA_HEADS):
        lo, hi = hd * LANES, (hd + 1) * LANES
        r_lo = MLA_WIDTH + hd * QK_ROPE_DIM
        q1, q2 = rope_t(q_t[r_lo:r_lo + QK_ROPE_DIM])
        qt_ref[0, hd, :QK_NOPE_DIM, :] = q_t[lo:hi].astype(BF16)
        qt_ref[0, hd, QK_NOPE_DIM:QK_NOPE_DIM + half, :] = q1.astype(BF16)
        qt_ref[0, hd, QK_NOPE_DIM + half:QK_HEAD_DIM, :] = q2.astype(BF16)
        qt_ref[0, hd, QK_HEAD_DIM:, :] = jnp.zeros((QK_PAD_DIM - QK_HEAD_DIM, rows), BF16)
        k_ref[0, hd, :, :QK_NOPE_DIM] = kn[:, lo:hi].astype(BF16)
        k_ref[0, hd, :, QK_NOPE_DIM:] = kpe_r
        vt_ref[0, hd, :V_HEAD_DIM, :] = vt[lo:hi, :].astype(BF16)
        vt_ref[0, hd, V_HEAD_DIM:, :] = jnp.ones((ONES_ROWS, rows), BF16)


def _mla_prep(x2, rope, layer, gpre, w_lat, gq, w_uq_t, gkv, w_k, w_vt, batch, seq):
    rows = PREP_ROWS
    per_seq = seq // rows
    half = QK_ROPE_DIM // 2
    head_rows = lambda i: (i // per_seq, 0, i % per_seq, 0)
    head_cols = lambda i: (i // per_seq, 0, 0, i % per_seq)
    return pl.pallas_call(
        _mla_prep_kernel,
        grid=(batch * seq // rows,),
        in_specs=[
            pl.BlockSpec((rows, D_MODEL), lambda i: (i, 0)),
            pl.BlockSpec((half, rows), lambda i: (0, i)),
            pl.BlockSpec((half, rows), lambda i: (0, i)),
            _layer_spec(gpre, layer),
            _layer_spec(w_lat, layer),
            _layer_spec(gq, layer),
            _layer_spec(w_uq_t, layer),
            _layer_spec(gkv, layer),
            _layer_spec(w_k, layer),
            _layer_spec(w_vt, layer),
        ],
        out_specs=[
            pl.BlockSpec((1, MLA_HEADS, QK_PAD_DIM, rows), head_cols),
            pl.BlockSpec((1, MLA_HEADS, rows, QK_PAD_DIM), head_rows),
            pl.BlockSpec((1, MLA_HEADS, VT_ROWS, rows), head_cols),
        ],
        out_shape=[
            jax.ShapeDtypeStruct((batch, MLA_HEADS, QK_PAD_DIM, seq), BF16),
            jax.ShapeDtypeStruct((batch, MLA_HEADS, seq, QK_PAD_DIM), BF16),
            jax.ShapeDtypeStruct((batch, MLA_HEADS, VT_ROWS, seq), BF16),
        ],
        compiler_params=pltpu.CompilerParams(dimension_semantics=("parallel",), vmem_limit_bytes=VMEM_LIMIT),
        name="mla_prep",
    )(x2, *rope, gpre, w_lat, gq, w_uq_t, gkv, w_k, w_vt)


def _mla_attn_kernel(qt_ref, k_ref, vt_ref, o_ref, s_buf, *, seq):
    n_blocks = seq // ATTN_Q
    n_chunks = seq // ATTN_K

    def q_block(j):
        return qt_ref[0, 0, :, pl.ds(pl.multiple_of(j * ATTN_Q, ATTN_Q), ATTN_Q)]

    def read_probs(c, m):
        return jnp.exp2((s_buf[c * ATTN_K:(c + 1) * ATTN_K, :] - m).astype(BF16))

    def write_scores(c, q_t):
        s = _dot(k_ref[0, 0, c * ATTN_K:(c + 1) * ATTN_K, :], q_t)
        s_buf[c * ATTN_K:(c + 1) * ATTN_K, :] = s
        return jnp.max(s, axis=0, keepdims=True)

    def weighted_values(probs):
        return _dot(vt_ref[0, 0], jnp.concatenate(probs, axis=0))

    def emit(j, acc):
        o_t = acc[:V_HEAD_DIM] * (1.0 / acc[V_HEAD_DIM:V_HEAD_DIM + 1])
        o_ref[0, pl.ds(pl.multiple_of(j * ATTN_Q, ATTN_Q), ATTN_Q), :] = o_t.T.astype(o_ref.dtype)

    def col_max(parts):
        return functools.reduce(jnp.maximum, parts)

    def step(j, carry):
        m_prev, acc_prev = carry
        emit(jnp.maximum(j - 2, 0), acc_prev)
        q_t = q_block(j)
        probs, maxes = [], []
        for c in range(n_chunks):
            probs.append(read_probs(c, m_prev))
            maxes.append(write_scores(c, q_t))
        return col_max(maxes), weighted_values(probs)

    q0 = q_block(0)
    m0 = col_max([write_scores(c, q0) for c in range(n_chunks)])
    m, acc = lax.fori_loop(1, n_blocks, step, (m0, jnp.ones((VT_ROWS, ATTN_Q), F32)))
    emit(n_blocks - 2, acc)
    emit(n_blocks - 1, weighted_values([read_probs(c, m) for c in range(n_chunks)]))


def _mla_attn(q_t, k, vt):
    batch, heads, seq, _ = k.shape
    return pl.pallas_call(
        functools.partial(_mla_attn_kernel, seq=seq),
        grid=(batch, heads),
        in_specs=[
            pl.BlockSpec((1, 1, QK_PAD_DIM, seq), lambda b, h: (b, h, 0, 0)),
            pl.BlockSpec((1, 1, seq, QK_PAD_DIM), lambda b, h: (b, h, 0, 0)),
            pl.BlockSpec((1, 1, VT_ROWS, seq), lambda b, h: (b, h, 0, 0)),
        ],
        out_specs=pl.BlockSpec((1, seq, V_HEAD_DIM), lambda b, h: (b, 0, h)),
        out_shape=jax.ShapeDtypeStruct((batch, seq, MLA_WIDTH), BF16),
        scratch_shapes=[pltpu.VMEM((seq, ATTN_Q), F32)],
        compiler_params=pltpu.CompilerParams(
            dimension_semantics=("parallel", "parallel"), vmem_limit_bytes=VMEM_LIMIT),
        name="mla_attn",
    )(q_t, k, vt)


def _mem_kv_kernel(mem_ref, g_ref, w_mk_ref, w_mv_ref, mk_ref, mv_ref):
    mem_n = _rms(mem_ref[0], g_ref[0]).astype(BF16)
    mk_ref[0, 0] = _dot(mem_n, w_mk_ref[0]).astype(BF16)
    mv_ref[0, 0] = _dot(mem_n, w_mv_ref[0]).astype(BF16)


def _mem_kv(mem, g, w_mk, w_mv):
    batch = mem.shape[0]
    out = jax.ShapeDtypeStruct((DEPTH, batch, MEM_TOKENS, MEM_WIDTH), BF16)
    return pl.pallas_call(
        _mem_kv_kernel,
        grid=(DEPTH, batch),
        in_specs=[
            pl.BlockSpec((1, MEM_TOKENS, D_MODEL), lambda l, b: (b, 0, 0)),
            pl.BlockSpec((1, 1, D_MODEL), lambda l, b: (l, 0, 0)),
            pl.BlockSpec((1, D_MODEL, MEM_WIDTH), lambda l, b: (l, 0, 0)),
            pl.BlockSpec((1, D_MODEL, MEM_WIDTH), lambda l, b: (l, 0, 0)),
        ],
        out_specs=[
            pl.BlockSpec((1, 1, MEM_TOKENS, MEM_WIDTH), lambda l, b: (l, b, 0, 0)),
            pl.BlockSpec((1, 1, MEM_TOKENS, MEM_WIDTH), lambda l, b: (l, b, 0, 0)),
        ],
        out_shape=[out, out],
        compiler_params=pltpu.CompilerParams(dimension_semantics=("parallel", "parallel")),
        name="mem_kv",
    )(mem, g, w_mk, w_mv)


def _mix_kernel(x_ref, xp_ref, xn_ref, a_ref, mk_ref, mv_ref, gpre_ref, w_ref, cw_ref, y_ref, *, seq):
    c_gc, c_qm = CONV_WIDTH, 3 * CONV_WIDTH
    rows = x_ref.shape[0]
    i = pl.program_id(0)
    keep_prev = ((i * rows) % seq != 0).astype(F32)
    keep_next = (((i + 1) * rows) % seq != 0).astype(F32)
    x_ext = jnp.concatenate([xp_ref[...] * keep_prev, x_ref[...], xn_ref[...] * keep_next], axis=0)
    h_ext = _rms(x_ext, gpre_ref[...]).astype(BF16)

    zc = _dot(h_ext, w_ref[:, c_gc:c_qm])
    u = zc[:, :CONV_WIDTH] * zc[:, CONV_WIDTH:]
    ext = rows + 2 * SUBLANES
    body = slice(SUBLANES, SUBLANES + rows)
    conv = (pltpu.roll(u, 1, 0)[body] * cw_ref[0:1, :] + u[body] * cw_ref[1:2, :]
            + pltpu.roll(u, ext - 1, 0)[body] * cw_ref[2:3, :])

    h = h_ext[body]
    c_gate = c_qm + MEM_WIDTH

    def gated(val, lo, hi):
        gate = _dot(h, w_ref[:, c_gate + lo:c_gate + hi])
        y_ref[:, lo:hi] = (val * (gate * (1.0 / (1.0 + jnp.exp(-gate))))).astype(BF16)

    gated(_dot(h, w_ref[:, :c_gc]) * conv, MLA_WIDTH, MLA_WIDTH + CONV_WIDTH)

    qm = _dot(h, w_ref[:, c_qm:c_gate]) * (MEM_HEAD_DIM ** -0.5)
    m_out = []
    for hd in range(MEM_HEADS):
        lo, hi = hd * MEM_HEAD_DIM, (hd + 1) * MEM_HEAD_DIM
        s = _dot_nt(qm[:, lo:hi].astype(BF16), mk_ref[0, :, lo:hi])
        p = jnp.exp(s - jnp.max(s, axis=-1, keepdims=True))
        m_out.append(_dot(p.astype(BF16), mv_ref[0, :, lo:hi]) * (1.0 / jnp.sum(p, axis=-1, keepdims=True)))
    gated(jnp.concatenate(m_out, axis=1), MLA_WIDTH + CONV_WIDTH, MIX_WIDTH)

    for lo in range(0, MLA_WIDTH, GATE_COLS):
        gated(a_ref[:, lo:lo + GATE_COLS].astype(F32), lo, lo + GATE_COLS)


def _mix(x2, a2, mk, mv, layer, gpre, w_rest, cw, batch, seq):
    rows = MIX_ROWS
    tokens = batch * seq
    per_seq = seq // rows
    tiles = rows // SUBLANES
    last_tile = tokens // SUBLANES - 1
    single = pl.Buffered(1)
    return pl.pallas_call(
        functools.partial(_mix_kernel, seq=seq),
        grid=(tokens // rows,),
        in_specs=[
            pl.BlockSpec((rows, D_MODEL), lambda i: (i, 0)),
            pl.BlockSpec((SUBLANES, D_MODEL), lambda i: (jnp.maximum(i * tiles - 1, 0), 0)),
            pl.BlockSpec((SUBLANES, D_MODEL), lambda i: (jnp.minimum((i + 1) * tiles, last_tile), 0)),
            pl.BlockSpec((rows, MLA_WIDTH), lambda i: (i, 0)),
            pl.BlockSpec((None, 1, MEM_TOKENS, MEM_WIDTH), lambda i: (layer, i // per_seq, 0, 0)),
            pl.BlockSpec((None, 1, MEM_TOKENS, MEM_WIDTH), lambda i: (layer, i // per_seq, 0, 0)),
            _layer_spec(gpre, layer),
            _layer_spec(w_rest, layer, pipeline_mode=single),
            _layer_spec(cw, layer),
        ],
        out_specs=pl.BlockSpec((rows, MIX_WIDTH), lambda i: (i, 0)),
        out_shape=jax.ShapeDtypeStruct((tokens, MIX_WIDTH), BF16),
        compiler_params=pltpu.CompilerParams(dimension_semantics=("parallel",), vmem_limit_bytes=VMEM_LIMIT),
        name="mix",
    )(x2, x2, x2, a2, mk, mv, gpre, w_rest, cw)


def _out_proj_kernel(y_ref, w_o_ref, x_ref, g_ref, o_ref):
    o = _dot(y_ref[...], w_o_ref[...])
    o_ref[...] = x_ref[...] + _rms(o, g_ref[...])


def _out_proj(y2, w_o, x2, g, layer):
    rows = OUT_ROWS
    tokens = x2.shape[0]
    return pl.pallas_call(
        _out_proj_kernel,
        grid=(tokens // rows,),
        in_specs=[
            pl.BlockSpec((rows, MIX_WIDTH), lambda i: (i, 0)),
            _layer_spec(w_o, layer, pipeline_mode=pl.Buffered(1)),
            pl.BlockSpec((rows, D_MODEL), lambda i: (i, 0)),
            _layer_spec(g, layer),
        ],
        out_specs=pl.BlockSpec((rows, D_MODEL), lambda i: (i, 0)),
        out_shape=jax.ShapeDtypeStruct(x2.shape, F32),
        compiler_params=pltpu.CompilerParams(dimension_semantics=("parallel",), vmem_limit_bytes=VMEM_LIMIT),
        name="out_proj",
    )(y2, w_o, x2, g)


def _rope_inv_freq_col():
    inv_freq = 1.0 / (ROPE_THETA ** (jnp.arange(0, QK_ROPE_DIM, 2, dtype=F32) / QK_ROPE_DIM))
    return inv_freq[:, None]


def kernel(x, mem, positions, pre_norm_g, w_in, q_norm_g, w_uq, kv_norm_g, w_ukv, conv_w, mem_norm_g, w_mk, w_mv,
           w_o, post_norm_g):
    batch, seq, _ = x.shape
    tokens = batch * seq
    assert seq % PREP_ROWS == 0 and seq % MIX_ROWS == 0 and seq % ATTN_Q == 0 and seq % ATTN_K == 0
    assert tokens % OUT_ROWS == 0

    c0 = Q_LORA_RANK + KV_LORA_RANK
    c1 = c0 + QK_ROPE_DIM
    w_lat = jnp.pad(w_in[:, :, :c1], ((0, 0), (0, 0), (0, LANES - QK_ROPE_DIM))).astype(BF16)
    w_rest = w_in[:, :, c1:].astype(BF16)

    w_uq_h = w_uq.reshape(DEPTH, Q_LORA_RANK, MLA_HEADS, QK_HEAD_DIM)
    w_uq_nope = w_uq_h[..., :QK_NOPE_DIM].reshape(DEPTH, Q_LORA_RANK, MLA_WIDTH)
    w_uq_rope = w_uq_h[..., QK_NOPE_DIM:].reshape(DEPTH, Q_LORA_RANK, MLA_HEADS * QK_ROPE_DIM)
    w_uq_t = jnp.swapaxes(jnp.concatenate([w_uq_nope, w_uq_rope], axis=-1), 1, 2).astype(BF16)

    w_ukv_h = w_ukv.reshape(DEPTH, KV_LORA_RANK, MLA_HEADS, QK_NOPE_DIM + V_HEAD_DIM)
    w_k = w_ukv_h[..., :QK_NOPE_DIM].reshape(DEPTH, KV_LORA_RANK, MLA_WIDTH).astype(BF16)
    w_vt = jnp.swapaxes(w_ukv_h[..., QK_NOPE_DIM:].reshape(DEPTH, KV_LORA_RANK, MLA_WIDTH), 1, 2).astype(BF16)

    w_o_b = w_o.astype(BF16)
    gpre, gq, gkv, gpost = (g[:, None, :] for g in (pre_norm_g, q_norm_g, kv_norm_g, post_norm_g))
    rope = _rope_tables(positions.reshape(1, tokens), _rope_inv_freq_col())
    mk, mv = _mem_kv(mem, mem_norm_g[:, None, :], w_mk.astype(BF16), w_mv.astype(BF16))

    x2 = x.reshape(tokens, D_MODEL)
    for l in range(DEPTH):
        q_t, k, vt = _mla_prep(x2, rope, l, gpre, w_lat, gq, w_uq_t, gkv, w_k, w_vt, batch, seq)
        a = _mla_attn(q_t, k, vt)
        y = _mix(x2, a.reshape(tokens, MLA_WIDTH), mk, mv, l, gpre, w_rest, conv_w, batch, seq)
        x2 = _out_proj(y, w_o_b, x2, gpost, l)
    return x2.reshape(batch, seq, D_MODEL)
```

```python
import functools
import math

import jax
import jax.numpy as jnp
from jax import lax
from jax.experimental import pallas as pl
from jax.experimental.pallas import tpu as pltpu

D_MODEL = 2048
DEPTH = 4
MEM_TOKENS = 256
EPS = 1e-6
ROPE_THETA = 10000.0
MLA_HEADS = 8
QK_NOPE_DIM = 128
QK_ROPE_DIM = 64
QK_HEAD_DIM = QK_NOPE_DIM + QK_ROPE_DIM
V_HEAD_DIM = 128
Q_LORA_RANK = 512
KV_LORA_RANK = 256
MLA_WIDTH = MLA_HEADS * V_HEAD_DIM
CONV_WIDTH = 512
MEM_HEADS = 4
MEM_HEAD_DIM = 128
MEM_WIDTH = MEM_HEADS * MEM_HEAD_DIM
MIX_WIDTH = MLA_WIDTH + CONV_WIDTH + MEM_WIDTH

LANES = 128
SUBLANES = 8
QK_PAD_DIM = QK_NOPE_DIM + LANES
ONES_ROWS = 2 * SUBLANES
VT_ROWS = V_HEAD_DIM + ONES_ROWS

BF16 = jnp.bfloat16
F32 = jnp.float32

PREP_ROWS = 512
ROPE_COLS = 2048
MIX_ROWS = 512
GATE_COLS = 512
OUT_ROWS = 512
ATTN_Q = 256
ATTN_K = 512
VMEM_LIMIT = 56 * 1024 * 1024

_NT = (((1,), (1,)), ((), ()))


def _rms(x, g):
    return x * lax.rsqrt(jnp.mean(x * x, axis=-1, keepdims=True) + EPS) * g


def _dot(a, b):
    return jnp.dot(a, b, preferred_element_type=F32)


def _dot_nt(a, b):
    return lax.dot_general(a, b, _NT, preferred_element_type=F32)


def _layer_spec(stacked, layer, **kwargs):
    return pl.BlockSpec((None,) + stacked.shape[1:], lambda i: (layer, 0, 0), **kwargs)


def _rope_tables_kernel(pos_ref, invf_ref, cos_t_ref, sin_t_ref):
    ang = invf_ref[...] * pos_ref[...].astype(F32)
    cos_t_ref[...] = jnp.cos(ang)
    sin_t_ref[...] = jnp.sin(ang)


def _rope_tables(pos_row, invf_col):
    tokens = pos_row.shape[1]
    cols = ROPE_COLS
    half = QK_ROPE_DIM // 2
    table = jax.ShapeDtypeStruct((half, tokens), F32)
    return pl.pallas_call(
        _rope_tables_kernel,
        grid=(tokens // cols,),
        in_specs=[pl.BlockSpec((1, cols), lambda i: (0, i)), pl.BlockSpec(invf_col.shape, lambda i: (0, 0))],
        out_specs=[pl.BlockSpec((half, cols), lambda i: (0, i)), pl.BlockSpec((half, cols), lambda i: (0, i))],
        out_shape=[table, table],
        compiler_params=pltpu.CompilerParams(dimension_semantics=("parallel",)),
        name="rope_tables",
    )(pos_row, invf_col)


def _mla_prep_kernel(x_ref, cos_t_ref, sin_t_ref, gpre_ref, w_lat_ref, gq_ref, w_uqt_ref, gkv_ref,
                     w_k_ref, w_vt_ref, qt_ref, k_ref, vt_ref):
    h = _rms(x_ref[...], gpre_ref[...]).astype(BF16)
    z = _dot_nt(h, w_lat_ref[...])
    qn = _rms(z[:, :Q_LORA_RANK], gq_ref[...]).astype(BF16)
    kvn = _rms(z[:, Q_LORA_RANK:Q_LORA_RANK + KV_LORA_RANK], gkv_ref[...]).astype(BF16)
    kpe_t = z[:, Q_LORA_RANK + KV_LORA_RANK:].T

    half = QK_ROPE_DIM // 2
    cos_t = cos_t_ref[...]
    sin_t = sin_t_ref[...]

    def rope_t(t):
        x1, x2 = t[:half], t[half:]
        return x1 * cos_t - x2 * sin_t, x2 * cos_t + x1 * sin_t

    rows = z.shape[0]
    k1, k2 = rope_t(kpe_t[:QK_ROPE_DIM])
    kpe_r = jnp.concatenate([k1, k2, jnp.zeros((LANES - QK_ROPE_DIM, rows), F32)], axis=0).T.astype(BF16)

    q_scale = (QK_HEAD_DIM ** -0.5) * math.log2(math.e)
    q_t = _dot_nt(w_uqt_ref[...], qn) * q_scale
    kn = _dot(kvn, w_k_ref[...])
    vt = _dot_nt(w_vt_ref[...], kvn)
    for hd in range(MLA_HEADS):
        lo, hi = hd * LANES, (hd + 1) * LANES
        r_lo = MLA_WIDTH + hd * QK_ROPE_DIM
        q1, q2 = rope_t(q_t[r_lo:r_lo + QK_ROPE_DIM])
        qt_ref[0, hd, :QK_NOPE_DIM, :] = q_t[lo:hi].astype(BF16)
        qt_ref[0, hd, QK_NOPE_DIM:QK_NOPE_DIM + half, :] = q1.astype(BF16)
        qt_ref[0, hd, QK_NOPE_DIM + half:QK_HEAD_DIM, :] = q2.astype(BF16)
        qt_ref[0, hd, QK_HEAD_DIM:, :] = jnp.zeros((QK_PAD_DIM - QK_HEAD_DIM, rows), BF16)
        k_ref[0, hd, :, :QK_NOPE_DIM] = kn[:, lo:hi].astype(BF16)
        k_ref[0, hd, :, QK_NOPE_DIM:] = kpe_r
        vt_ref[0, hd, :V_HEAD_DIM, :] = vt[lo:hi, :].astype(BF16)
        vt_ref[0, hd, V_HEAD_DIM:, :] = jnp.ones((ONES_ROWS, rows), BF16)


def _mla_prep(x2, rope, layer, gpre, w_lat, gq, w_uq_t, gkv, w_k, w_vt, batch, seq):
    rows = PREP_ROWS
    per_seq = seq // rows
    half = QK_ROPE_DIM // 2
    head_rows = lambda i: (i // per_seq, 0, i % per_seq, 0)
    head_cols = lambda i: (i // per_seq, 0, 0, i % per_seq)
    return pl.pallas_call(
        _mla_prep_kernel,
        grid=(batch * seq // rows,),
        in_specs=[
            pl.BlockSpec((rows, D_MODEL), lambda i: (i, 0)),
            pl.BlockSpec((half, rows), lambda i: (0, i)),
            pl.BlockSpec((half, rows), lambda i: (0, i)),
            _layer_spec(gpre, layer),
            _layer_spec(w_lat, layer),
            _layer_spec(gq, layer),
            _layer_spec(w_uq_t, layer),
            _layer_spec(gkv, layer),
            _layer_spec(w_k, layer),
            _layer_spec(w_vt, layer),
        ],
        out_specs=[
            pl.BlockSpec((1, MLA_HEADS, QK_PAD_DIM, rows), head_cols),
            pl.BlockSpec((1, MLA_HEADS, rows, QK_PAD_DIM), head_rows),
            pl.BlockSpec((1, MLA_HEADS, VT_ROWS, rows), head_cols),
        ],
        out_shape=[
            jax.ShapeDtypeStruct((batch, MLA_HEADS, QK_PAD_DIM, seq), BF16),
            jax.ShapeDtypeStruct((batch, MLA_HEADS, seq, QK_PAD_DIM), BF16),
            jax.ShapeDtypeStruct((batch, MLA_HEADS, VT_ROWS, seq), BF16),
        ],
        compiler_params=pltpu.CompilerParams(dimension_semantics=("parallel",), vmem_limit_bytes=VMEM_LIMIT),
        name="mla_prep",
    )(x2, *rope, gpre, w_lat, gq, w_uq_t, gkv, w_k, w_vt)


def _mla_attn_kernel(qt_ref, k_ref, vt_ref, o_ref, s_buf, *, seq):
    n_blocks = seq // ATTN_Q
    n_chunks = seq // ATTN_K

    def q_block(j):
        return qt_ref[0, 0, :, pl.ds(pl.multiple_of(j * ATTN_Q, ATTN_Q), ATTN_Q)]

    def read_probs(c, m):
        return jnp.exp2((s_buf[c * ATTN_K:(c + 1) * ATTN_K, :] - m).astype(BF16))

    def write_scores(c, q_t):
        s = _dot(k_ref[0, 0, c * ATTN_K:(c + 1) * ATTN_K, :], q_t)
        s_buf[c * ATTN_K:(c + 1) * ATTN_K, :] = s
        return jnp.max(s, axis=0, keepdims=True)

    def weighted_values(probs):
        return _dot(vt_ref[0, 0], jnp.concatenate(probs, axis=0))

    def emit(j, acc):
        o_t = acc[:V_HEAD_DIM] * (1.0 / acc[V_HEAD_DIM:V_HEAD_DIM + 1])
        o_ref[0, pl.ds(pl.multiple_of(j * ATTN_Q, ATTN_Q), ATTN_Q), :] = o_t.T.astype(o_ref.dtype)

    def col_max(parts):
        return functools.reduce(jnp.maximum, parts)

    def step(j, carry):
        m_prev, acc_prev = carry
        emit(jnp.maximum(j - 2, 0), acc_prev)
        q_t = q_block(j)
        probs, maxes = [], []
        for c in range(n_chunks):
            probs.append(read_probs(c, m_prev))
            maxes.append(write_scores(c, q_t))
        return col_max(maxes), weighted_values(probs)

    q0 = q_block(0)
    m0 = col_max([write_scores(c, q0) for c in range(n_chunks)])
    m, acc = lax.fori_loop(1, n_blocks, step, (m0, jnp.ones((VT_ROWS, ATTN_Q), F32)))
    emit(n_blocks - 2, acc)
    emit(n_blocks - 1, weighted_values([read_probs(c, m) for c in range(n_chunks)]))


def _mla_attn(q_t, k, vt):
    batch, heads, seq, _ = k.shape
    return pl.pallas_call(
        functools.partial(_mla_attn_kernel, seq=seq),
        grid=(batch, heads),
        in_specs=[
            pl.BlockSpec((1, 1, QK_PAD_DIM, seq), lambda b, h: (b, h, 0, 0)),
            pl.BlockSpec((1, 1, seq, QK_PAD_DIM), lambda b, h: (b, h, 0, 0)),
            pl.BlockSpec((1, 1, VT_ROWS, seq), lambda b, h: (b, h, 0, 0)),
        ],
        out_specs=pl.BlockSpec((1, seq, V_HEAD_DIM), lambda b, h: (b, 0, h)),
        out_shape=jax.ShapeDtypeStruct((batch, seq, MLA_WIDTH), BF16),
        scratch_shapes=[pltpu.VMEM((seq, ATTN_Q), F32)],
        compiler_params=pltpu.CompilerParams(
            dimension_semantics=("parallel", "parallel"), vmem_limit_bytes=VMEM_LIMIT),
        name="mla_attn",
    )(q_t, k, vt)


def _mem_kv_kernel(mem_ref, g_ref, w_mk_ref, w_mv_ref, mk_ref, mv_ref):
    mem_n = _rms(mem_ref[0], g_ref[0]).astype(BF16)
    mk_ref[0, 0] = _dot(mem_n, w_mk_ref[0]).astype(BF16)
    mv_ref[0, 0] = _dot(mem_n, w_mv_ref[0]).astype(BF16)


def _mem_kv(mem, g, w_mk, w_mv):
    batch = mem.shape[0]
    out = jax.ShapeDtypeStruct((DEPTH, batch, MEM_TOKENS, MEM_WIDTH), BF16)
    return pl.pallas_call(
        _mem_kv_kernel,
        grid=(DEPTH, batch),
        in_specs=[
            pl.BlockSpec((1, MEM_TOKENS, D_MODEL), lambda l, b: (b, 0, 0)),
            pl.BlockSpec((1, 1, D_MODEL), lambda l, b: (l, 0, 0)),
            pl.BlockSpec((1, D_MODEL, MEM_WIDTH), lambda l, b: (l, 0, 0)),
            pl.BlockSpec((1, D_MODEL, MEM_WIDTH), lambda l, b: (l, 0, 0)),
        ],
        out_specs=[
            pl.BlockSpec((1, 1, MEM_TOKENS, MEM_WIDTH), lambda l, b: (l, b, 0, 0)),
            pl.BlockSpec((1, 1, MEM_TOKENS, MEM_WIDTH), lambda l, b: (l, b, 0, 0)),
        ],
        out_shape=[out, out],
        compiler_params=pltpu.CompilerParams(dimension_semantics=("parallel", "parallel")),
        name="mem_kv",
    )(mem, g, w_mk, w_mv)


def _mix_kernel(x_ref, xp_ref, xn_ref, a_ref, mk_ref, mv_ref, gpre_ref, w_ref, cw_ref, y_ref, *, seq):
    c_gc, c_qm = CONV_WIDTH, 3 * CONV_WIDTH
    rows = x_ref.shape[0]
    i = pl.program_id(0)
    keep_prev = ((i * rows) % seq != 0).astype(F32)
    keep_next = (((i + 1) * rows) % seq != 0).astype(F32)
    x_ext = jnp.concatenate([xp_ref[...] * keep_prev, x_ref[...], xn_ref[...] * keep_next], axis=0)
    h_ext = _rms(x_ext, gpre_ref[...]).astype(BF16)

    zc = _dot_nt(h_ext, w_ref[c_gc:c_qm, :])
    u = zc[:, :CONV_WIDTH] * zc[:, CONV_WIDTH:]
    ext = rows + 2 * SUBLANES
    body = slice(SUBLANES, SUBLANES + rows)
    conv = (pltpu.roll(u, 1, 0)[body] * cw_ref[0:1, :] + u[body] * cw_ref[1:2, :]
            + pltpu.roll(u, ext - 1, 0)[body] * cw_ref[2:3, :])

    h = h_ext[body]
    c_gate = c_qm + MEM_WIDTH

    def gated(val, lo, hi):
        gate = _dot_nt(h, w_ref[c_gate + lo:c_gate + hi, :])
        y_ref[:, lo:hi] = (val * (gate * (1.0 / (1.0 + jnp.exp(-gate))))).astype(BF16)

    gated(_dot_nt(h, w_ref[:c_gc, :]) * conv, MLA_WIDTH, MLA_WIDTH + CONV_WIDTH)

    qm = _dot_nt(h, w_ref[c_qm:c_gate, :]) * (MEM_HEAD_DIM ** -0.5)
    m_out = []
    for hd in range(MEM_HEADS):
        lo, hi = hd * MEM_HEAD_DIM, (hd + 1) * MEM_HEAD_DIM
        s = _dot_nt(qm[:, lo:hi].astype(BF16), mk_ref[0, :, lo:hi])
        p = jnp.exp(s - jnp.max(s, axis=-1, keepdims=True))
        m_out.append(_dot(p.astype(BF16), mv_ref[0, :, lo:hi]) * (1.0 / jnp.sum(p, axis=-1, keepdims=True)))
    gated(jnp.concatenate(m_out, axis=1), MLA_WIDTH + CONV_WIDTH, MIX_WIDTH)

    for lo in range(0, MLA_WIDTH, GATE_COLS):
        gated(a_ref[:, lo:lo + GATE_COLS].astype(F32), lo, lo + GATE_COLS)


def _mix(x2, a2, mk, mv, layer, gpre, w_rest, cw, batch, seq):
    rows = MIX_ROWS
    tokens = batch * seq
    per_seq = seq // rows
    tiles = rows // SUBLANES
    last_tile = tokens // SUBLANES - 1
    single = pl.Buffered(1)
    return pl.pallas_call(
        functools.partial(_mix_kernel, seq=seq),
        grid=(tokens // rows,),
        in_specs=[
            pl.BlockSpec((rows, D_MODEL), lambda i: (i, 0)),
            pl.BlockSpec((SUBLANES, D_MODEL), lambda i: (jnp.maximum(i * tiles - 1, 0), 0)),
            pl.BlockSpec((SUBLANES, D_MODEL), lambda i: (jnp.minimum((i + 1) * tiles, last_tile), 0)),
            pl.BlockSpec((rows, MLA_WIDTH), lambda i: (i, 0)),
            pl.BlockSpec((None, 1, MEM_TOKENS, MEM_WIDTH), lambda i: (layer, i // per_seq, 0, 0)),
            pl.BlockSpec((None, 1, MEM_TOKENS, MEM_WIDTH), lambda i: (layer, i // per_seq, 0, 0)),
            _layer_spec(gpre, layer),
            _layer_spec(w_rest, layer, pipeline_mode=single),
            _layer_spec(cw, layer),
        ],
        out_specs=pl.BlockSpec((rows, MIX_WIDTH), lambda i: (i, 0)),
        out_shape=jax.ShapeDtypeStruct((tokens, MIX_WIDTH), BF16),
        compiler_params=pltpu.CompilerParams(dimension_semantics=("parallel",), vmem_limit_bytes=VMEM_LIMIT),
        name="mix",
    )(x2, x2, x2, a2, mk, mv, gpre, w_rest, cw)


def _out_proj_kernel(y_ref, w_o_ref, x_ref, g_ref, o_ref):
    o = _dot(y_ref[...], w_o_ref[...])
    o_ref[...] = x_ref[...] + _rms(o, g_ref[...])


def _out_proj(y2, w_o, x2, g, layer):
    rows = OUT_ROWS
    tokens = x2.shape[0]
    return pl.pallas_call(
        _out_proj_kernel,
        grid=(tokens // rows,),
        in_specs=[
            pl.BlockSpec((rows, MIX_WIDTH), lambda i: (i, 0)),
            _layer_spec(w_o, layer, pipeline_mode=pl.Buffered(1)),
            pl.BlockSpec((rows, D_MODEL), lambda i: (i, 0)),
            _layer_spec(g, layer),
        ],
        out_specs=pl.BlockSpec((rows, D_MODEL), lambda i: (i, 0)),
        out_shape=jax.ShapeDtypeStruct(x2.shape, F32),
        compiler_params=pltpu.CompilerParams(dimension_semantics=("parallel",), vmem_limit_bytes=VMEM_LIMIT),
        name="out_proj",
    )(y2, w_o, x2, g)


def _rope_inv_freq_col():
    inv_freq = 1.0 / (ROPE_THETA ** (jnp.arange(0, QK_ROPE_DIM, 2, dtype=F32) / QK_ROPE_DIM))
    return inv_freq[:, None]


def kernel(x, mem, positions, pre_norm_g, w_in, q_norm_g, w_uq, kv_norm_g, w_ukv, conv_w, mem_norm_g, w_mk, w_mv,
           w_o, post_norm_g):
    batch, seq, _ = x.shape
    tokens = batch * seq
    assert seq % PREP_ROWS == 0 and seq % MIX_ROWS == 0 and seq % ATTN_Q == 0 and seq % ATTN_K == 0
    assert tokens % OUT_ROWS == 0

    c1 = Q_LORA_RANK + KV_LORA_RANK + QK_ROPE_DIM
    w_in_t = jnp.swapaxes(w_in, 1, 2)
    w_lat = jnp.pad(w_in_t[:, :c1, :], ((0, 0), (0, LANES - QK_ROPE_DIM), (0, 0))).astype(BF16)
    w_rest = w_in_t[:, c1:, :].astype(BF16)

    w_uq_h = w_uq.reshape(DEPTH, Q_LORA_RANK, MLA_HEADS, QK_HEAD_DIM)
    w_uq_nope = w_uq_h[..., :QK_NOPE_DIM].reshape(DEPTH, Q_LORA_RANK, MLA_WIDTH)
    w_uq_rope = w_uq_h[..., QK_NOPE_DIM:].reshape(DEPTH, Q_LORA_RANK, MLA_HEADS * QK_ROPE_DIM)
    w_uq_t = jnp.swapaxes(jnp.concatenate([w_uq_nope, w_uq_rope], axis=-1), 1, 2).astype(BF16)

    w_ukv_h = w_ukv.reshape(DEPTH, KV_LORA_RANK, MLA_HEADS, QK_NOPE_DIM + V_HEAD_DIM)
    w_k = w_ukv_h[..., :QK_NOPE_DIM].reshape(DEPTH, KV_LORA_RANK, MLA_WIDTH).astype(BF16)
    w_vt = jnp.swapaxes(w_ukv_h[..., QK_NOPE_DIM:].reshape(DEPTH, KV_LORA_RANK, MLA_WIDTH), 1, 2).astype(BF16)

    w_o_b = w_o.astype(BF16)
    gpre, gq, gkv, gpost = (g[:, None, :] for g in (pre_norm_g, q_norm_g, kv_norm_g, post_norm_g))
    rope = _rope_tables(positions.reshape(1, tokens), _rope_inv_freq_col())
    mk, mv = _mem_kv(mem, mem_norm_g[:, None, :], w_mk.astype(BF16), w_mv.astype(BF16))

    x2 = x.reshape(tokens, D_MODEL)
    for l in range(DEPTH):
        q_t, k, vt = _mla_prep(x2, rope, l, gpre, w_lat, gq, w_uq_t, gkv, w_k, w_vt, batch, seq)
        a = _mla_attn(q_t, k, vt)
        y = _mix(x2, a.reshape(tokens, MLA_WIDTH), mk, mv, l, gpre, w_rest, conv_w, batch, seq)
        x2 = _out_proj(y, w_o_b, x2, gpost, l)
    return x2.reshape(batch, seq, D_MODEL)
```

```python
import functools
import math

import jax
import jax.numpy as jnp
from jax import lax
from jax.experimental import pallas as pl
from jax.experimental.pallas import tpu as pltpu

D_MODEL = 2048
DEPTH = 4
MEM_TOKENS = 256
EPS = 1e-6
ROPE_THETA = 10000.0
MLA_HEADS = 8
QK_NOPE_DIM = 128
QK_ROPE_DIM = 64
QK_HEAD_DIM = QK_NOPE_DIM + QK_ROPE_DIM
V_HEAD_DIM = 128
Q_LORA_RANK = 512
KV_LORA_RANK = 256
MLA_WIDTH = MLA_HEADS * V_HEAD_DIM
CONV_WIDTH = 512
MEM_HEADS = 4
MEM_HEAD_DIM = 128
MEM_WIDTH = MEM_HEADS * MEM_HEAD_DIM
MIX_WIDTH = MLA_WIDTH + CONV_WIDTH + MEM_WIDTH

LANES = 128
SUBLANES = 8
QK_PAD_DIM = QK_NOPE_DIM + LANES
ONES_ROWS = 2 * SUBLANES
HALO = 2 * SUBLANES
VT_ROWS = V_HEAD_DIM + ONES_ROWS

BF16 = jnp.bfloat16
F32 = jnp.float32

PREP_ROWS = 512
ROPE_COLS = 2048
MIX_ROWS = 512
GATE_COLS = 512
OUT_ROWS = 512
ATTN_Q = 256
ATTN_K = 512
ATTN_HEADS = 2
VMEM_LIMIT = 56 * 1024 * 1024

_NT = (((1,), (1,)), ((), ()))


def _rms(x, g):
    return x * lax.rsqrt(jnp.mean(x * x, axis=-1, keepdims=True) + EPS) * g


def _dot(a, b):
    return jnp.dot(a, b, preferred_element_type=F32)


def _dot_nt(a, b):
    return lax.dot_general(a, b, _NT, preferred_element_type=F32)


def _layer_spec(stacked, layer, **kwargs):
    return pl.BlockSpec((None,) + stacked.shape[1:], lambda i: (layer, 0, 0), **kwargs)


def _rope_tables_kernel(pos_ref, invf_ref, cos_t_ref, sin_t_ref):
    ang = invf_ref[...] * pos_ref[...].astype(F32)
    cos_t_ref[...] = jnp.cos(ang)
    sin_t_ref[...] = jnp.sin(ang)


def _rope_tables(pos_row, invf_col):
    tokens = pos_row.shape[1]
    cols = ROPE_COLS
    half = QK_ROPE_DIM // 2
    table = jax.ShapeDtypeStruct((half, tokens), F32)
    return pl.pallas_call(
        _rope_tables_kernel,
        grid=(tokens // cols,),
        in_specs=[pl.BlockSpec((1, cols), lambda i: (0, i)), pl.BlockSpec(invf_col.shape, lambda i: (0, 0))],
        out_specs=[pl.BlockSpec((half, cols), lambda i: (0, i)), pl.BlockSpec((half, cols), lambda i: (0, i))],
        out_shape=[table, table],
        compiler_params=pltpu.CompilerParams(dimension_semantics=("parallel",)),
        name="rope_tables",
    )(pos_row, invf_col)


def _mla_prep_kernel(x_ref, cos_t_ref, sin_t_ref, gpre_ref, w_lat_ref, gq_ref, w_uqt_ref, gkv_ref,
                     w_k_ref, w_vt_ref, qt_ref, k_ref, vt_ref, h_ref):
    h = _rms(x_ref[...], gpre_ref[...]).astype(BF16)
    h_ref[...] = h
    z = _dot_nt(h, w_lat_ref[...])
    qn = _rms(z[:, :Q_LORA_RANK], gq_ref[...]).astype(BF16)
    kvn = _rms(z[:, Q_LORA_RANK:Q_LORA_RANK + KV_LORA_RANK], gkv_ref[...]).astype(BF16)
    kpe_t = z[:, Q_LORA_RANK + KV_LORA_RANK:].T

    half = QK_ROPE_DIM // 2
    cos_t = cos_t_ref[...]
    sin_t = sin_t_ref[...]

    def rope_t(t):
        x1, x2 = t[:half], t[half:]
        return x1 * cos_t - x2 * sin_t, x2 * cos_t + x1 * sin_t

    rows = z.shape[0]
    k1, k2 = rope_t(kpe_t[:QK_ROPE_DIM])
    kpe_r = jnp.concatenate([k1, k2, jnp.zeros((LANES - QK_ROPE_DIM, rows), F32)], axis=0).T.astype(BF16)

    q_scale = (QK_HEAD_DIM ** -0.5) * math.log2(math.e)
    q_t = _dot_nt(w_uqt_ref[...], qn) * q_scale
    kn = _dot(kvn, w_k_ref[...])
    vt = _dot_nt(w_vt_ref[...], kvn)
    for hd in range(MLA_HEADS):
        lo, hi = hd * LANES, (hd + 1) * LANES
        r_lo = MLA_WIDTH + hd * QK_ROPE_DIM
        q1, q2 = rope_t(q_t[r_lo:r_lo + QK_ROPE_DIM])
        qt_ref[0, hd, :QK_NOPE_DIM, :] = q_t[lo:hi].astype(BF16)
        qt_ref[0, hd, QK_NOPE_DIM:QK_NOPE_DIM + half, :] = q1.astype(BF16)
        qt_ref[0, hd, QK_NOPE_DIM + half:QK_HEAD_DIM, :] = q2.astype(BF16)
        qt_ref[0, hd, QK_HEAD_DIM:, :] = jnp.zeros((QK_PAD_DIM - QK_HEAD_DIM, rows), BF16)
        k_ref[0, hd, :, :QK_NOPE_DIM] = kn[:, lo:hi].astype(BF16)
        k_ref[0, hd, :, QK_NOPE_DIM:] = kpe_r
        vt_ref[0, hd, :V_HEAD_DIM, :] = vt[lo:hi, :].astype(BF16)
        vt_ref[0, hd, V_HEAD_DIM:, :] = jnp.ones((ONES_ROWS, rows), BF16)


def _mla_prep(x2, rope, layer, gpre, w_lat, gq, w_uq_t, gkv, w_k, w_vt, batch, seq):
    rows = PREP_ROWS
    per_seq = seq // rows
    half = QK_ROPE_DIM // 2
    head_rows = lambda i: (i // per_seq, 0, i % per_seq, 0)
    head_cols = lambda i: (i // per_seq, 0, 0, i % per_seq)
    return pl.pallas_call(
        _mla_prep_kernel,
        grid=(batch * seq // rows,),
        in_specs=[
            pl.BlockSpec((rows, D_MODEL), lambda i: (i, 0)),
            pl.BlockSpec((half, rows), lambda i: (0, i)),
            pl.BlockSpec((half, rows), lambda i: (0, i)),
            _layer_spec(gpre, layer),
            _layer_spec(w_lat, layer),
            _layer_spec(gq, layer),
            _layer_spec(w_uq_t, layer),
            _layer_spec(gkv, layer),
            _layer_spec(w_k, layer),
            _layer_spec(w_vt, layer),
        ],
        out_specs=[
            pl.BlockSpec((1, MLA_HEADS, QK_PAD_DIM, rows), head_cols),
            pl.BlockSpec((1, MLA_HEADS, rows, QK_PAD_DIM), head_rows),
            pl.BlockSpec((1, MLA_HEADS, VT_ROWS, rows), head_cols),
            pl.BlockSpec((rows, D_MODEL), lambda i: (i, 0)),
        ],
        out_shape=[
            jax.ShapeDtypeStruct((batch, MLA_HEADS, QK_PAD_DIM, seq), BF16),
            jax.ShapeDtypeStruct((batch, MLA_HEADS, seq, QK_PAD_DIM), BF16),
            jax.ShapeDtypeStruct((batch, MLA_HEADS, VT_ROWS, seq), BF16),
            jax.ShapeDtypeStruct((batch * seq, D_MODEL), BF16),
        ],
        compiler_params=pltpu.CompilerParams(dimension_semantics=("parallel",), vmem_limit_bytes=VMEM_LIMIT),
        name="mla_prep",
    )(x2, *rope, gpre, w_lat, gq, w_uq_t, gkv, w_k, w_vt)


def _mla_attn_kernel(qt_ref, k_ref, vt_ref, o_ref, *s_bufs, seq):
    n_blocks = seq // ATTN_Q
    n_chunks = seq // ATTN_K

    def q_block(hd, j):
        return qt_ref[0, hd, :, pl.ds(pl.multiple_of(j * ATTN_Q, ATTN_Q), ATTN_Q)]

    def read_probs(hd, c, m):
        return jnp.exp2((s_bufs[hd][c * ATTN_K:(c + 1) * ATTN_K, :] - m).astype(BF16))

    def write_scores(hd, c, q_t):
        s = _dot(k_ref[0, hd, c * ATTN_K:(c + 1) * ATTN_K, :], q_t)
        s_bufs[hd][c * ATTN_K:(c + 1) * ATTN_K, :] = s
        return jnp.max(s, axis=0, keepdims=True)

    def weighted_values(hd, c, m, acc):
        part = _dot(vt_ref[0, hd, :, c * ATTN_K:(c + 1) * ATTN_K], read_probs(hd, c, m))
        return part if acc is None else acc + part

    def emit(hd, j, acc):
        o_t = acc[:V_HEAD_DIM] * (1.0 / acc[V_HEAD_DIM:V_HEAD_DIM + 1])
        rows = pl.ds(pl.multiple_of(j * ATTN_Q, ATTN_Q), ATTN_Q)
        o_ref[0, rows, hd * V_HEAD_DIM:(hd + 1) * V_HEAD_DIM] = o_t.T.astype(o_ref.dtype)

    def col_max(parts):
        return functools.reduce(jnp.maximum, parts)

    def first_scores(hd):
        q_t = q_block(hd, 0)
        return col_max([write_scores(hd, c, q_t) for c in range(n_chunks)])

    def step(hd, j, carry):
        m_prev, acc_prev = carry
        emit(hd, jnp.maximum(j - 2, 0), acc_prev)
        q_t = q_block(hd, j)
        acc, maxes = None, []
        for c in range(n_chunks):
            acc = weighted_values(hd, c, m_prev, acc)
            maxes.append(write_scores(hd, c, q_t))
        return col_max(maxes), acc

    def all_heads(j, carries):
        return tuple(step(hd, j, carry) for hd, carry in enumerate(carries))

    init = tuple((first_scores(hd), jnp.ones((VT_ROWS, ATTN_Q), F32)) for hd in range(ATTN_HEADS))
    carries = lax.fori_loop(1, n_blocks, all_heads, init)
    for hd, (m, acc) in enumerate(carries):
        emit(hd, n_blocks - 2, acc)
        acc = None
        for c in range(n_chunks):
            acc = weighted_values(hd, c, m, acc)
        emit(hd, n_blocks - 1, acc)


def _mla_attn(q_t, k, vt):
    batch, heads, seq, _ = k.shape
    group = ATTN_HEADS
    return pl.pallas_call(
        functools.partial(_mla_attn_kernel, seq=seq),
        grid=(batch, heads // group),
        in_specs=[
            pl.BlockSpec((1, group, QK_PAD_DIM, seq), lambda b, g: (b, g, 0, 0)),
            pl.BlockSpec((1, group, seq, QK_PAD_DIM), lambda b, g: (b, g, 0, 0)),
            pl.BlockSpec((1, group, VT_ROWS, seq), lambda b, g: (b, g, 0, 0)),
        ],
        out_specs=pl.BlockSpec((1, seq, group * V_HEAD_DIM), lambda b, g: (b, 0, g)),
        out_shape=jax.ShapeDtypeStruct((batch, seq, MLA_WIDTH), BF16),
        scratch_shapes=[pltpu.VMEM((seq, ATTN_Q), F32)] * group,
        compiler_params=pltpu.CompilerParams(
            dimension_semantics=("parallel", "parallel"), vmem_limit_bytes=VMEM_LIMIT),
        name="mla_attn",
    )(q_t, k, vt)


def _mem_kv_kernel(mem_ref, g_ref, w_mk_ref, w_mv_ref, mk_ref, mv_ref):
    mem_n = _rms(mem_ref[0], g_ref[0]).astype(BF16)
    mk_ref[0, 0] = _dot(mem_n, w_mk_ref[0]).astype(BF16)
    mv_ref[0, 0] = _dot(mem_n, w_mv_ref[0]).astype(BF16)


def _mem_kv(mem, g, w_mk, w_mv):
    batch = mem.shape[0]
    out = jax.ShapeDtypeStruct((DEPTH, batch, MEM_TOKENS, MEM_WIDTH), BF16)
    return pl.pallas_call(
        _mem_kv_kernel,
        grid=(DEPTH, batch),
        in_specs=[
            pl.BlockSpec((1, MEM_TOKENS, D_MODEL), lambda l, b: (b, 0, 0)),
            pl.BlockSpec((1, 1, D_MODEL), lambda l, b: (l, 0, 0)),
            pl.BlockSpec((1, D_MODEL, MEM_WIDTH), lambda l, b: (l, 0, 0)),
            pl.BlockSpec((1, D_MODEL, MEM_WIDTH), lambda l, b: (l, 0, 0)),
        ],
        out_specs=[
            pl.BlockSpec((1, 1, MEM_TOKENS, MEM_WIDTH), lambda l, b: (l, b, 0, 0)),
            pl.BlockSpec((1, 1, MEM_TOKENS, MEM_WIDTH), lambda l, b: (l, b, 0, 0)),
        ],
        out_shape=[out, out],
        compiler_params=pltpu.CompilerParams(dimension_semantics=("parallel", "parallel")),
        name="mem_kv",
    )(mem, g, w_mk, w_mv)


def _mix_kernel(h_ref, hp_ref, hn_ref, a_ref, mk_ref, mv_ref, w_ref, cw_ref, y_ref, *, seq):
    c_gc, c_qm = CONV_WIDTH, 3 * CONV_WIDTH
    rows = h_ref.shape[0]
    i = pl.program_id(0)
    keep_prev = ((i * rows) % seq != 0).astype(BF16)
    keep_next = (((i + 1) * rows) % seq != 0).astype(BF16)
    h_ext = jnp.concatenate([hp_ref[...] * keep_prev, h_ref[...], hn_ref[...] * keep_next], axis=0)

    zc = _dot_nt(h_ext, w_ref[c_gc:c_qm, :])
    u = zc[:, :CONV_WIDTH] * zc[:, CONV_WIDTH:]
    ext = rows + 2 * HALO
    body = slice(HALO, HALO + rows)
    conv = (pltpu.roll(u, 1, 0)[body] * cw_ref[0:1, :] + u[body] * cw_ref[1:2, :]
            + pltpu.roll(u, ext - 1, 0)[body] * cw_ref[2:3, :])

    h = h_ref[...]
    c_gate = c_qm + MEM_WIDTH

    def gated(val, lo, hi):
        gate = _dot_nt(h, w_ref[c_gate + lo:c_gate + hi, :])
        y_ref[:, lo:hi] = (val * (gate * (1.0 / (1.0 + jnp.exp(-gate))))).astype(BF16)

    gated(_dot_nt(h, w_ref[:c_gc, :]) * conv, MLA_WIDTH, MLA_WIDTH + CONV_WIDTH)

    qm = _dot_nt(h, w_ref[c_qm:c_gate, :]) * (MEM_HEAD_DIM ** -0.5)
    m_out = []
    for hd in range(MEM_HEADS):
        lo, hi = hd * MEM_HEAD_DIM, (hd + 1) * MEM_HEAD_DIM
        s = _dot_nt(qm[:, lo:hi].astype(BF16), mk_ref[0, :, lo:hi])
        p = jnp.exp(s - jnp.max(s, axis=-1, keepdims=True))
        m_out.append(_dot(p.astype(BF16), mv_ref[0, :, lo:hi]) * (1.0 / jnp.sum(p, axis=-1, keepdims=True)))
    gated(jnp.concatenate(m_out, axis=1), MLA_WIDTH + CONV_WIDTH, MIX_WIDTH)

    for lo in range(0, MLA_WIDTH, GATE_COLS):
        gated(a_ref[:, lo:lo + GATE_COLS].astype(F32), lo, lo + GATE_COLS)


def _mix(h2, a2, mk, mv, layer, w_rest, cw, batch, seq):
    rows = MIX_ROWS
    tokens = batch * seq
    per_seq = seq // rows
    tiles = rows // HALO
    last_tile = tokens // HALO - 1
    single = pl.Buffered(1)
    return pl.pallas_call(
        functools.partial(_mix_kernel, seq=seq),
        grid=(tokens // rows,),
        in_specs=[
            pl.BlockSpec((rows, D_MODEL), lambda i: (i, 0)),
            pl.BlockSpec((HALO, D_MODEL), lambda i: (jnp.maximum(i * tiles - 1, 0), 0)),
            pl.BlockSpec((HALO, D_MODEL), lambda i: (jnp.minimum((i + 1) * tiles, last_tile), 0)),
            pl.BlockSpec((rows, MLA_WIDTH), lambda i: (i, 0)),
            pl.BlockSpec((None, 1, MEM_TOKENS, MEM_WIDTH), lambda i: (layer, i // per_seq, 0, 0)),
            pl.BlockSpec((None, 1, MEM_TOKENS, MEM_WIDTH), lambda i: (layer, i // per_seq, 0, 0)),
            _layer_spec(w_rest, layer, pipeline_mode=single),
            _layer_spec(cw, layer),
        ],
        out_specs=pl.BlockSpec((rows, MIX_WIDTH), lambda i: (i, 0)),
        out_shape=jax.ShapeDtypeStruct((tokens, MIX_WIDTH), BF16),
        compiler_params=pltpu.CompilerParams(dimension_semantics=("parallel",), vmem_limit_bytes=VMEM_LIMIT),
        name="mix",
    )(h2, h2, h2, a2, mk, mv, w_rest, cw)


def _out_proj_kernel(y_ref, w_o_ref, x_ref, g_ref, o_ref):
    o = _dot(y_ref[...], w_o_ref[...])
    o_ref[...] = x_ref[...] + _rms(o, g_ref[...])


def _out_proj(y2, w_o, x2, g, layer):
    rows = OUT_ROWS
    tokens = x2.shape[0]
    return pl.pallas_call(
        _out_proj_kernel,
        grid=(tokens // rows,),
        in_specs=[
            pl.BlockSpec((rows, MIX_WIDTH), lambda i: (i, 0)),
            _layer_spec(w_o, layer, pipeline_mode=pl.Buffered(1)),
            pl.BlockSpec((rows, D_MODEL), lambda i: (i, 0)),
            _layer_spec(g, layer),
        ],
        out_specs=pl.BlockSpec((rows, D_MODEL), lambda i: (i, 0)),
        out_shape=jax.ShapeDtypeStruct(x2.shape, F32),
        compiler_params=pltpu.CompilerParams(dimension_semantics=("parallel",), vmem_limit_bytes=VMEM_LIMIT),
        name="out_proj",
    )(y2, w_o, x2, g)


def _rope_inv_freq_col():
    inv_freq = 1.0 / (ROPE_THETA ** (jnp.arange(0, QK_ROPE_DIM, 2, dtype=F32) / QK_ROPE_DIM))
    return inv_freq[:, None]


def kernel(x, mem, positions, pre_norm_g, w_in, q_norm_g, w_uq, kv_norm_g, w_ukv, conv_w, mem_norm_g, w_mk, w_mv,
           w_o, post_norm_g):
    batch, seq, _ = x.shape
    tokens = batch * seq
    assert seq % PREP_ROWS == 0 and seq % MIX_ROWS == 0 and seq % ATTN_Q == 0 and seq % ATTN_K == 0
    assert tokens % OUT_ROWS == 0

    c1 = Q_LORA_RANK + KV_LORA_RANK + QK_ROPE_DIM
    w_in_t = jnp.swapaxes(w_in, 1, 2)
    w_lat = jnp.pad(w_in_t[:, :c1, :], ((0, 0), (0, LANES - QK_ROPE_DIM), (0, 0))).astype(BF16)
    w_rest = w_in_t[:, c1:, :].astype(BF16)

    w_uq_h = w_uq.reshape(DEPTH, Q_LORA_RANK, MLA_HEADS, QK_HEAD_DIM)
    w_uq_nope = w_uq_h[..., :QK_NOPE_DIM].reshape(DEPTH, Q_LORA_RANK, MLA_WIDTH)
    w_uq_rope = w_uq_h[..., QK_NOPE_DIM:].reshape(DEPTH, Q_LORA_RANK, MLA_HEADS * QK_ROPE_DIM)
    w_uq_t = jnp.swapaxes(jnp.concatenate([w_uq_nope, w_uq_rope], axis=-1), 1, 2).astype(BF16)

    w_ukv_h = w_ukv.reshape(DEPTH, KV_LORA_RANK, MLA_HEADS, QK_NOPE_DIM + V_HEAD_DIM)
    w_k = w_ukv_h[..., :QK_NOPE_DIM].reshape(DEPTH, KV_LORA_RANK, MLA_WIDTH).astype(BF16)
    w_vt = jnp.swapaxes(w_ukv_h[..., QK_NOPE_DIM:].reshape(DEPTH, KV_LORA_RANK, MLA_WIDTH), 1, 2).astype(BF16)

    w_o_b = w_o.astype(BF16)
    gpre, gq, gkv, gpost = (g[:, None, :] for g in (pre_norm_g, q_norm_g, kv_norm_g, post_norm_g))
    rope = _rope_tables(positions.reshape(1, tokens), _rope_inv_freq_col())
    mk, mv = _mem_kv(mem, mem_norm_g[:, None, :], w_mk.astype(BF16), w_mv.astype(BF16))

    x2 = x.reshape(tokens, D_MODEL)
    for l in range(DEPTH):
        q_t, k, vt, h2 = _mla_prep(x2, rope, l, gpre, w_lat, gq, w_uq_t, gkv, w_k, w_vt, batch, seq)
        a = _mla_attn(q_t, k, vt)
        y = _mix(h2, a.reshape(tokens, MLA_WIDTH), mk, mv, l, w_rest, conv_w, batch, seq)
        x2 = _out_proj(y, w_o_b, x2, gpost, l)
    return x2.reshape(batch, seq, D_MODEL)
```

```python
import functools
import math

import jax
import jax.numpy as jnp
from jax import lax
from jax.experimental import pallas as pl
from jax.experimental.pallas import tpu as pltpu

D_MODEL = 2048
DEPTH = 4
MEM_TOKENS = 256
EPS = 1e-6
ROPE_THETA = 10000.0
MLA_HEADS = 8
QK_NOPE_DIM = 128
QK_ROPE_DIM = 64
QK_HEAD_DIM = QK_NOPE_DIM + QK_ROPE_DIM
V_HEAD_DIM = 128
Q_LORA_RANK = 512
KV_LORA_RANK = 256
MLA_WIDTH = MLA_HEADS * V_HEAD_DIM
CONV_WIDTH = 512
MEM_HEADS = 4
MEM_HEAD_DIM = 128
MEM_WIDTH = MEM_HEADS * MEM_HEAD_DIM
MIX_WIDTH = MLA_WIDTH + CONV_WIDTH + MEM_WIDTH

LANES = 128
SUBLANES = 8
QK_PAD_DIM = QK_NOPE_DIM + LANES
ONES_ROWS = 2 * SUBLANES
HALO = 2 * SUBLANES
VT_ROWS = V_HEAD_DIM + ONES_ROWS

BF16 = jnp.bfloat16
F32 = jnp.float32

PREP_ROWS = 512
ROPE_COLS = 2048
MIX_ROWS = 512
GATE_COLS = 512
OUT_ROWS = 512
ATTN_Q = 256
ATTN_K = 512
ATTN_HEADS = 2
VMEM_LIMIT = 56 * 1024 * 1024

_NT = (((1,), (1,)), ((), ()))


def _rms(x, g):
    return x * lax.rsqrt(jnp.mean(x * x, axis=-1, keepdims=True) + EPS) * g


def _dot(a, b):
    return jnp.dot(a, b, preferred_element_type=F32)


def _dot_nt(a, b):
    return lax.dot_general(a, b, _NT, preferred_element_type=F32)


def _layer_spec(stacked, layer, **kwargs):
    return pl.BlockSpec((None,) + stacked.shape[1:], lambda i: (layer, 0, 0), **kwargs)


def _rope_tables_kernel(pos_ref, invf_ref, cos_t_ref, sin_t_ref):
    ang = invf_ref[...] * pos_ref[...].astype(F32)
    cos_t_ref[...] = jnp.cos(ang)
    sin_t_ref[...] = jnp.sin(ang)


def _rope_tables(pos_row, invf_col):
    tokens = pos_row.shape[1]
    cols = ROPE_COLS
    half = QK_ROPE_DIM // 2
    table = jax.ShapeDtypeStruct((half, tokens), F32)
    return pl.pallas_call(
        _rope_tables_kernel,
        grid=(tokens // cols,),
        in_specs=[pl.BlockSpec((1, cols), lambda i: (0, i)), pl.BlockSpec(invf_col.shape, lambda i: (0, 0))],
        out_specs=[pl.BlockSpec((half, cols), lambda i: (0, i)), pl.BlockSpec((half, cols), lambda i: (0, i))],
        out_shape=[table, table],
        compiler_params=pltpu.CompilerParams(dimension_semantics=("parallel",)),
        name="rope_tables",
    )(pos_row, invf_col)


def _mla_prep_kernel(x_ref, cos_t_ref, sin_t_ref, gpre_ref, w_lat_ref, gq_ref, w_uqt_ref, gkv_ref,
                     w_k_ref, w_vt_ref, qt_ref, k_ref, vt_ref, h_ref):
    h = _rms(x_ref[...], gpre_ref[...]).astype(BF16)
    h_ref[...] = h
    z = _dot_nt(h, w_lat_ref[...])
    qn = _rms(z[:, :Q_LORA_RANK], gq_ref[...]).astype(BF16)
    kvn = _rms(z[:, Q_LORA_RANK:Q_LORA_RANK + KV_LORA_RANK], gkv_ref[...]).astype(BF16)
    kpe_t = z[:, Q_LORA_RANK + KV_LORA_RANK:].T

    half = QK_ROPE_DIM // 2
    cos_t = cos_t_ref[...]
    sin_t = sin_t_ref[...]

    def rope_t(t):
        x1, x2 = t[:half], t[half:]
        return x1 * cos_t - x2 * sin_t, x2 * cos_t + x1 * sin_t

    rows = z.shape[0]
    k1, k2 = rope_t(kpe_t[:QK_ROPE_DIM])
    kpe_r = jnp.concatenate([k1, k2, jnp.zeros((LANES - QK_ROPE_DIM, rows), F32)], axis=0).T.astype(BF16)

    q_scale = (QK_HEAD_DIM ** -0.5) * math.log2(math.e)
    q_t = _dot_nt(w_uqt_ref[...], qn) * q_scale
    kn = _dot(kvn, w_k_ref[...])
    vt = _dot_nt(w_vt_ref[...], kvn)
    for hd in range(MLA_HEADS):
        lo, hi = hd * LANES, (hd + 1) * LANES
        r_lo = MLA_WIDTH + hd * QK_ROPE_DIM
        q1, q2 = rope_t(q_t[r_lo:r_lo + QK_ROPE_DIM])
        qt_ref[0, hd, :QK_NOPE_DIM, :] = q_t[lo:hi].astype(BF16)
        qt_ref[0, hd, QK_NOPE_DIM:QK_NOPE_DIM + half, :] = q1.astype(BF16)
        qt_ref[0, hd, QK_NOPE_DIM + half:QK_HEAD_DIM, :] = q2.astype(BF16)
        qt_ref[0, hd, QK_HEAD_DIM:, :] = jnp.zeros((QK_PAD_DIM - QK_HEAD_DIM, rows), BF16)
        k_ref[0, hd, :, :QK_NOPE_DIM] = kn[:, lo:hi].astype(BF16)
        k_ref[0, hd, :, QK_NOPE_DIM:] = kpe_r
        vt_ref[0, hd, :V_HEAD_DIM, :] = vt[lo:hi, :].astype(BF16)
        vt_ref[0, hd, V_HEAD_DIM:, :] = jnp.ones((ONES_ROWS, rows), BF16)


def _mla_prep(x2, rope, layer, gpre, w_lat, gq, w_uq_t, gkv, w_k, w_vt, batch, seq):
    rows = PREP_ROWS
    per_seq = seq // rows
    half = QK_ROPE_DIM // 2
    head_rows = lambda i: (i // per_seq, 0, i % per_seq, 0)
    head_cols = lambda i: (i // per_seq, 0, 0, i % per_seq)
    return pl.pallas_call(
        _mla_prep_kernel,
        grid=(batch * seq // rows,),
        in_specs=[
            pl.BlockSpec((rows, D_MODEL), lambda i: (i, 0)),
            pl.BlockSpec((half, rows), lambda i: (0, i)),
            pl.BlockSpec((half, rows), lambda i: (0, i)),
            _layer_spec(gpre, layer),
            _layer_spec(w_lat, layer),
            _layer_spec(gq, layer),
            _layer_spec(w_uq_t, layer),
            _layer_spec(gkv, layer),
            _layer_spec(w_k, layer),
            _layer_spec(w_vt, layer),
        ],
        out_specs=[
            pl.BlockSpec((1, MLA_HEADS, QK_PAD_DIM, rows), head_cols),
            pl.BlockSpec((1, MLA_HEADS, rows, QK_PAD_DIM), head_rows),
            pl.BlockSpec((1, MLA_HEADS, VT_ROWS, rows), head_cols),
            pl.BlockSpec((rows, D_MODEL), lambda i: (i, 0)),
        ],
        out_shape=[
            jax.ShapeDtypeStruct((batch, MLA_HEADS, QK_PAD_DIM, seq), BF16),
            jax.ShapeDtypeStruct((batch, MLA_HEADS, seq, QK_PAD_DIM), BF16),
            jax.ShapeDtypeStruct((batch, MLA_HEADS, VT_ROWS, seq), BF16),
            jax.ShapeDtypeStruct((batch * seq, D_MODEL), BF16),
        ],
        compiler_params=pltpu.CompilerParams(dimension_semantics=("parallel",), vmem_limit_bytes=VMEM_LIMIT),
        name="mla_prep",
    )(x2, *rope, gpre, w_lat, gq, w_uq_t, gkv, w_k, w_vt)


def _mla_attn_kernel(qt_ref, k_ref, vt_ref, o_ref, *s_bufs, seq):
    n_blocks = seq // ATTN_Q
    n_chunks = seq // ATTN_K

    def q_block(hd, j):
        return qt_ref[0, hd, :, pl.ds(pl.multiple_of(j * ATTN_Q, ATTN_Q), ATTN_Q)]

    def read_probs(hd, c, m):
        return jnp.exp2((s_bufs[hd][c * ATTN_K:(c + 1) * ATTN_K, :] - m).astype(BF16))

    def write_scores(hd, c, q_t):
        s = _dot(k_ref[0, hd, c * ATTN_K:(c + 1) * ATTN_K, :], q_t)
        s_bufs[hd][c * ATTN_K:(c + 1) * ATTN_K, :] = s
        return jnp.max(s, axis=0, keepdims=True)

    def weighted_values(hd, c, m, acc):
        part = _dot(vt_ref[0, hd, :, c * ATTN_K:(c + 1) * ATTN_K], read_probs(hd, c, m))
        return part if acc is None else acc + part

    def emit(hd, j, acc):
        o_t = acc[:V_HEAD_DIM] * (1.0 / acc[V_HEAD_DIM:V_HEAD_DIM + 1])
        rows = pl.ds(pl.multiple_of(j * ATTN_Q, ATTN_Q), ATTN_Q)
        o_ref[0, rows, hd * V_HEAD_DIM:(hd + 1) * V_HEAD_DIM] = o_t.T.astype(o_ref.dtype)

    def col_max(parts):
        return functools.reduce(jnp.maximum, parts)

    def first_scores(hd):
        q_t = q_block(hd, 0)
        return col_max([write_scores(hd, c, q_t) for c in range(n_chunks)])

    def step(hd, j, carry):
        m_prev, acc_prev = carry
        emit(hd, jnp.maximum(j - 2, 0), acc_prev)
        q_t = q_block(hd, j)
        acc, maxes = None, []
        for c in range(n_chunks):
            acc = weighted_values(hd, c, m_prev, acc)
            maxes.append(write_scores(hd, c, q_t))
        return col_max(maxes), acc

    def all_heads(j, carries):
        return tuple(step(hd, j, carry) for hd, carry in enumerate(carries))

    init = tuple((first_scores(hd), jnp.ones((VT_ROWS, ATTN_Q), F32)) for hd in range(ATTN_HEADS))
    carries = lax.fori_loop(1, n_blocks, all_heads, init)
    for hd, (m, acc) in enumerate(carries):
        emit(hd, n_blocks - 2, acc)
        acc = None
        for c in range(n_chunks):
            acc = weighted_values(hd, c, m, acc)
        emit(hd, n_blocks - 1, acc)


def _mla_attn(q_t, k, vt):
    batch, heads, seq, _ = k.shape
    group = ATTN_HEADS
    return pl.pallas_call(
        functools.partial(_mla_attn_kernel, seq=seq),
        grid=(batch, heads // group),
        in_specs=[
            pl.BlockSpec((1, group, QK_PAD_DIM, seq), lambda b, g: (b, g, 0, 0)),
            pl.BlockSpec((1, group, seq, QK_PAD_DIM), lambda b, g: (b, g, 0, 0)),
            pl.BlockSpec((1, group, VT_ROWS, seq), lambda b, g: (b, g, 0, 0)),
        ],
        out_specs=pl.BlockSpec((1, seq, group * V_HEAD_DIM), lambda b, g: (b, 0, g)),
        out_shape=jax.ShapeDtypeStruct((batch, seq, MLA_WIDTH), BF16),
        scratch_shapes=[pltpu.VMEM((seq, ATTN_Q), F32)] * group,
        compiler_params=pltpu.CompilerParams(
            dimension_semantics=("parallel", "parallel"), vmem_limit_bytes=VMEM_LIMIT),
        name="mla_attn",
    )(q_t, k, vt)


def _mem_kv_kernel(mem_ref, g_ref, w_mk_ref, w_mv_ref, mk_ref, mv_ref):
    mem_n = _rms(mem_ref[0], g_ref[0]).astype(BF16)
    mk_ref[0, 0] = _dot(mem_n, w_mk_ref[0]).astype(BF16)
    mv_ref[0, 0] = _dot(mem_n, w_mv_ref[0]).astype(BF16)


def _mem_kv(mem, g, w_mk, w_mv):
    batch = mem.shape[0]
    out = jax.ShapeDtypeStruct((DEPTH, batch, MEM_TOKENS, MEM_WIDTH), BF16)
    return pl.pallas_call(
        _mem_kv_kernel,
        grid=(DEPTH, batch),
        in_specs=[
            pl.BlockSpec((1, MEM_TOKENS, D_MODEL), lambda l, b: (b, 0, 0)),
            pl.BlockSpec((1, 1, D_MODEL), lambda l, b: (l, 0, 0)),
            pl.BlockSpec((1, D_MODEL, MEM_WIDTH), lambda l, b: (l, 0, 0)),
            pl.BlockSpec((1, D_MODEL, MEM_WIDTH), lambda l, b: (l, 0, 0)),
        ],
        out_specs=[
            pl.BlockSpec((1, 1, MEM_TOKENS, MEM_WIDTH), lambda l, b: (l, b, 0, 0)),
            pl.BlockSpec((1, 1, MEM_TOKENS, MEM_WIDTH), lambda l, b: (l, b, 0, 0)),
        ],
        out_shape=[out, out],
        compiler_params=pltpu.CompilerParams(dimension_semantics=("parallel", "parallel")),
        name="mem_kv",
    )(mem, g, w_mk, w_mv)


def _mix_kernel(h_ref, hp_ref, hn_ref, a_ref, mk_ref, mv_ref, w_ref, cw_ref, y_ref, *, seq):
    c_gc, c_qm = CONV_WIDTH, 3 * CONV_WIDTH
    rows = h_ref.shape[0]
    i = pl.program_id(0)
    keep_prev = ((i * rows) % seq != 0).astype(BF16)
    keep_next = (((i + 1) * rows) % seq != 0).astype(BF16)
    h_ext = jnp.concatenate([hp_ref[...] * keep_prev, h_ref[...], hn_ref[...] * keep_next], axis=0)

    zc = _dot_nt(h_ext, w_ref[c_gc:c_qm, :])
    u = zc[:, :CONV_WIDTH] * zc[:, CONV_WIDTH:]
    ext = rows + 2 * HALO
    body = slice(HALO, HALO + rows)
    conv = (pltpu.roll(u, 1, 0)[body] * cw_ref[0:1, :] + u[body] * cw_ref[1:2, :]
            + pltpu.roll(u, ext - 1, 0)[body] * cw_ref[2:3, :])

    h = h_ref[...]
    c_gate = c_qm + MEM_WIDTH

    def gated(val, lo, hi):
        gate = _dot_nt(h, w_ref[c_gate + lo:c_gate + hi, :])
        y_ref[:, lo:hi] = (val * (gate * (1.0 / (1.0 + jnp.exp(-gate))))).astype(BF16)

    gated(_dot_nt(h, w_ref[:c_gc, :]) * conv, MLA_WIDTH, MLA_WIDTH + CONV_WIDTH)

    qm = _dot_nt(h, w_ref[c_qm:c_gate, :]) * (MEM_HEAD_DIM ** -0.5)
    m_out = []
    for hd in range(MEM_HEADS):
        lo, hi = hd * MEM_HEAD_DIM, (hd + 1) * MEM_HEAD_DIM
        s = _dot_nt(qm[:, lo:hi].astype(BF16), mk_ref[0, :, lo:hi])
        p = jnp.exp(s - jnp.max(s, axis=-1, keepdims=True))
        m_out.append(_dot(p.astype(BF16), mv_ref[0, :, lo:hi]) * (1.0 / jnp.sum(p, axis=-1, keepdims=True)))
    gated(jnp.concatenate(m_out, axis=1), MLA_WIDTH + CONV_WIDTH, MIX_WIDTH)

    for lo in range(0, MLA_WIDTH, GATE_COLS):
        gated(a_ref[:, lo:lo + GATE_COLS].astype(F32), lo, lo + GATE_COLS)


def _mix(h2, a2, mk, mv, layer, w_rest, cw, batch, seq):
    rows = MIX_ROWS
    tokens = batch * seq
    per_seq = seq // rows
    tiles = rows // HALO
    last_tile = tokens // HALO - 1
    single = pl.Buffered(1)
    return pl.pallas_call(
        functools.partial(_mix_kernel, seq=seq),
        grid=(tokens // rows,),
        in_specs=[
            pl.BlockSpec((rows, D_MODEL), lambda i: (i, 0)),
            pl.BlockSpec((HALO, D_MODEL), lambda i: (jnp.maximum(i * tiles - 1, 0), 0)),
            pl.BlockSpec((HALO, D_MODEL), lambda i: (jnp.minimum((i + 1) * tiles, last_tile), 0)),
            pl.BlockSpec((rows, MLA_WIDTH), lambda i: (i, 0)),
            pl.BlockSpec((None, 1, MEM_TOKENS, MEM_WIDTH), lambda i: (layer, i // per_seq, 0, 0)),
            pl.BlockSpec((None, 1, MEM_TOKENS, MEM_WIDTH), lambda i: (layer, i // per_seq, 0, 0)),
            _layer_spec(w_rest, layer, pipeline_mode=single),
            _layer_spec(cw, layer),
        ],
        out_specs=pl.BlockSpec((rows, MIX_WIDTH), lambda i: (i, 0)),
        out_shape=jax.ShapeDtypeStruct((tokens, MIX_WIDTH), BF16),
        compiler_params=pltpu.CompilerParams(dimension_semantics=("parallel",), vmem_limit_bytes=VMEM_LIMIT),
        name="mix",
    )(h2, h2, h2, a2, mk, mv, w_rest, cw)


def _out_proj_kernel(y_ref, w_o_ref, x_ref, g_ref, o_ref):
    o = _dot(y_ref[...], w_o_ref[...])
    o_ref[...] = x_ref[...] + _rms(o, g_ref[...])


def _out_proj(y2, w_o, x2, g, layer):
    rows = OUT_ROWS
    tokens = x2.shape[0]
    return pl.pallas_call(
        _out_proj_kernel,
        grid=(tokens // rows,),
        in_specs=[
            pl.BlockSpec((rows, MIX_WIDTH), lambda i: (i, 0)),
            _layer_spec(w_o, layer, pipeline_mode=pl.Buffered(1)),
            pl.BlockSpec((rows, D_MODEL), lambda i: (i, 0)),
            _layer_spec(g, layer),
        ],
        out_specs=pl.BlockSpec((rows, D_MODEL), lambda i: (i, 0)),
        out_shape=jax.ShapeDtypeStruct(x2.shape, F32),
        compiler_params=pltpu.CompilerParams(dimension_semantics=("parallel",), vmem_limit_bytes=VMEM_LIMIT),
        name="out_proj",
    )(y2, w_o, x2, g)


def _rope_inv_freq_col():
    inv_freq = 1.0 / (ROPE_THETA ** (jnp.arange(0, QK_ROPE_DIM, 2, dtype=F32) / QK_ROPE_DIM))
    return inv_freq[:, None]


def kernel(x, mem, positions, pre_norm_g, w_in, q_norm_g, w_uq, kv_norm_g, w_ukv, conv_w, mem_norm_g, w_mk, w_mv,
           w_o, post_norm_g):
    batch, seq, _ = x.shape
    tokens = batch * seq
    assert seq % PREP_ROWS == 0 and seq % MIX_ROWS == 0 and seq % ATTN_Q == 0 and seq % ATTN_K == 0
    assert tokens % OUT_ROWS == 0 and MLA_HEADS % ATTN_HEADS == 0

    c1 = Q_LORA_RANK + KV_LORA_RANK + QK_ROPE_DIM
    w_in_t = jnp.swapaxes(w_in, 1, 2)
    w_lat = jnp.pad(w_in_t[:, :c1, :], ((0, 0), (0, LANES - QK_ROPE_DIM), (0, 0))).astype(BF16)
    w_rest = w_in_t[:, c1:, :].astype(BF16)

    w_uq_h = w_uq.reshape(DEPTH, Q_LORA_RANK, MLA_HEADS, QK_HEAD_DIM)
    w_uq_nope = w_uq_h[..., :QK_NOPE_DIM].reshape(DEPTH, Q_LORA_RANK, MLA_WIDTH)
    w_uq_rope = w_uq_h[..., QK_NOPE_DIM:].reshape(DEPTH, Q_LORA_RANK, MLA_HEADS * QK_ROPE_DIM)
    w_uq_t = jnp.swapaxes(jnp.concatenate([w_uq_nope, w_uq_rope], axis=-1), 1, 2).astype(BF16)

    w_ukv_h = w_ukv.reshape(DEPTH, KV_LORA_RANK, MLA_HEADS, QK_NOPE_DIM + V_HEAD_DIM)
    w_k = w_ukv_h[..., :QK_NOPE_DIM].reshape(DEPTH, KV_LORA_RANK, MLA_WIDTH).astype(BF16)
    w_vt = jnp.swapaxes(w_ukv_h[..., QK_NOPE_DIM:].reshape(DEPTH, KV_LORA_RANK, MLA_WIDTH), 1, 2).astype(BF16)

    w_o_b = w_o.astype(BF16)
    gpre, gq, gkv, gpost = (g[:, None, :] for g in (pre_norm_g, q_norm_g, kv_norm_g, post_norm_g))
    rope = _rope_tables(positions.reshape(1, tokens), _rope_inv_freq_col())
    mk, mv = _mem_kv(mem, mem_norm_g[:, None, :], w_mk.astype(BF16), w_mv.astype(BF16))

    x2 = x.reshape(tokens, D_MODEL)
    for l in range(DEPTH):
        q_t, k, vt, h2 = _mla_prep(x2, rope, l, gpre, w_lat, gq, w_uq_t, gkv, w_k, w_vt, batch, seq)
        a = _mla_attn(q_t, k, vt)
        y = _mix(h2, a.reshape(tokens, MLA_WIDTH), mk, mv, l, w_rest, conv_w, batch, seq)
        x2 = _out_proj(y, w_o_b, x2, gpost, l)
    return x2.reshape(batch, seq, D_MODEL)
```

```python
import functools
import math

import jax
import jax.numpy as jnp
from jax import lax
from jax.experimental import pallas as pl
from jax.experimental.pallas import tpu as pltpu

D_MODEL = 2048
DEPTH = 4
MEM_TOKENS = 256
EPS = 1e-6
ROPE_THETA = 10000.0
MLA_HEADS = 8
QK_NOPE_DIM = 128
QK_ROPE_DIM = 64
QK_HEAD_DIM = QK_NOPE_DIM + QK_ROPE_DIM
V_HEAD_DIM = 128
Q_LORA_RANK = 512
KV_LORA_RANK = 256
MLA_WIDTH = MLA_HEADS * V_HEAD_DIM
CONV_WIDTH = 512
MEM_HEADS = 4
MEM_HEAD_DIM = 128
MEM_WIDTH = MEM_HEADS * MEM_HEAD_DIM
MIX_WIDTH = MLA_WIDTH + CONV_WIDTH + MEM_WIDTH

LANES = 128
SUBLANES = 8
QK_PAD_DIM = QK_NOPE_DIM + LANES
ONES_ROWS = 2 * SUBLANES
HALO = 2 * SUBLANES
VT_ROWS = V_HEAD_DIM + ONES_ROWS

BF16 = jnp.bfloat16
F32 = jnp.float32

PREP_ROWS = 512
ROPE_COLS = 2048
MIX_ROWS = 1024
GATE_COLS = 512
OUT_ROWS = 512
ATTN_Q = 256
ATTN_K = 512
ATTN_HEADS = 2
VMEM_LIMIT = 56 * 1024 * 1024

_NT = (((1,), (1,)), ((), ()))


def _rms(x, g):
    return x * lax.rsqrt(jnp.mean(x * x, axis=-1, keepdims=True) + EPS) * g


def _dot(a, b):
    return jnp.dot(a, b, preferred_element_type=F32)


def _dot_nt(a, b):
    return lax.dot_general(a, b, _NT, preferred_element_type=F32)


def _layer_spec(stacked, layer, **kwargs):
    return pl.BlockSpec((None,) + stacked.shape[1:], lambda i: (layer, 0, 0), **kwargs)


def _rope_tables_kernel(pos_ref, invf_ref, cos_t_ref, sin_t_ref):
    ang = invf_ref[...] * pos_ref[...].astype(F32)
    cos_t_ref[...] = jnp.cos(ang)
    sin_t_ref[...] = jnp.sin(ang)


def _rope_tables(pos_row, invf_col):
    tokens = pos_row.shape[1]
    cols = ROPE_COLS
    half = QK_ROPE_DIM // 2
    table = jax.ShapeDtypeStruct((half, tokens), F32)
    return pl.pallas_call(
        _rope_tables_kernel,
        grid=(tokens // cols,),
        in_specs=[pl.BlockSpec((1, cols), lambda i: (0, i)), pl.BlockSpec(invf_col.shape, lambda i: (0, 0))],
        out_specs=[pl.BlockSpec((half, cols), lambda i: (0, i)), pl.BlockSpec((half, cols), lambda i: (0, i))],
        out_shape=[table, table],
        compiler_params=pltpu.CompilerParams(dimension_semantics=("parallel",)),
        name="rope_tables",
    )(pos_row, invf_col)


def _mla_prep_kernel(x_ref, cos_t_ref, sin_t_ref, gpre_ref, w_lat_ref, gq_ref, w_uqt_ref, gkv_ref,
                     w_k_ref, w_vt_ref, qt_ref, k_ref, vt_ref, h_ref):
    h = _rms(x_ref[...], gpre_ref[...]).astype(BF16)
    h_ref[...] = h
    z = _dot_nt(h, w_lat_ref[...])
    qn = _rms(z[:, :Q_LORA_RANK], gq_ref[...]).astype(BF16)
    kvn = _rms(z[:, Q_LORA_RANK:Q_LORA_RANK + KV_LORA_RANK], gkv_ref[...]).astype(BF16)
    kpe_t = z[:, Q_LORA_RANK + KV_LORA_RANK:].T

    half = QK_ROPE_DIM // 2
    cos_t = cos_t_ref[...]
    sin_t = sin_t_ref[...]

    def rope_t(t):
        x1, x2 = t[:half], t[half:]
        return x1 * cos_t - x2 * sin_t, x2 * cos_t + x1 * sin_t

    rows = z.shape[0]
    k1, k2 = rope_t(kpe_t[:QK_ROPE_DIM])
    kpe_r = jnp.concatenate([k1, k2, jnp.zeros((LANES - QK_ROPE_DIM, rows), F32)], axis=0).T.astype(BF16)

    q_scale = (QK_HEAD_DIM ** -0.5) * math.log2(math.e)
    q_t = _dot_nt(w_uqt_ref[...], qn) * q_scale
    kn = _dot(kvn, w_k_ref[...])
    vt = _dot_nt(w_vt_ref[...], kvn)
    for hd in range(MLA_HEADS):
        lo, hi = hd * LANES, (hd + 1) * LANES
        r_lo = MLA_WIDTH + hd * QK_ROPE_DIM
        q1, q2 = rope_t(q_t[r_lo:r_lo + QK_ROPE_DIM])
        qt_ref[0, hd, :QK_NOPE_DIM, :] = q_t[lo:hi].astype(BF16)
        qt_ref[0, hd, QK_NOPE_DIM:QK_NOPE_DIM + half, :] = q1.astype(BF16)
        qt_ref[0, hd, QK_NOPE_DIM + half:QK_HEAD_DIM, :] = q2.astype(BF16)
        qt_ref[0, hd, QK_HEAD_DIM:, :] = jnp.zeros((QK_PAD_DIM - QK_HEAD_DIM, rows), BF16)
        k_ref[0, hd, :, :QK_NOPE_DIM] = kn[:, lo:hi].astype(BF16)
        k_ref[0, hd, :, QK_NOPE_DIM:] = kpe_r
        vt_ref[0, hd, :V_HEAD_DIM, :] = vt[lo:hi, :].astype(BF16)
        vt_ref[0, hd, V_HEAD_DIM:, :] = jnp.ones((ONES_ROWS, rows), BF16)


def _mla_prep(x2, rope, layer, gpre, w_lat, gq, w_uq_t, gkv, w_k, w_vt, batch, seq):
    rows = PREP_ROWS
    per_seq = seq // rows
    half = QK_ROPE_DIM // 2
    head_rows = lambda i: (i // per_seq, 0, i % per_seq, 0)
    head_cols = lambda i: (i // per_seq, 0, 0, i % per_seq)
    return pl.pallas_call(
        _mla_prep_kernel,
        grid=(batch * seq // rows,),
        in_specs=[
            pl.BlockSpec((rows, D_MODEL), lambda i: (i, 0)),
            pl.BlockSpec((half, rows), lambda i: (0, i)),
            pl.BlockSpec((half, rows), lambda i: (0, i)),
            _layer_spec(gpre, layer),
            _layer_spec(w_lat, layer),
            _layer_spec(gq, layer),
            _layer_spec(w_uq_t, layer),
            _layer_spec(gkv, layer),
            _layer_spec(w_k, layer),
            _layer_spec(w_vt, layer),
        ],
        out_specs=[
            pl.BlockSpec((1, MLA_HEADS, QK_PAD_DIM, rows), head_cols),
            pl.BlockSpec((1, MLA_HEADS, rows, QK_PAD_DIM), head_rows),
            pl.BlockSpec((1, MLA_HEADS, VT_ROWS, rows), head_cols),
            pl.BlockSpec((rows, D_MODEL), lambda i: (i, 0)),
        ],
        out_shape=[
            jax.ShapeDtypeStruct((batch, MLA_HEADS, QK_PAD_DIM, seq), BF16),
            jax.ShapeDtypeStruct((batch, MLA_HEADS, seq, QK_PAD_DIM), BF16),
            jax.ShapeDtypeStruct((batch, MLA_HEADS, VT_ROWS, seq), BF16),
            jax.ShapeDtypeStruct((batch * seq, D_MODEL), BF16),
        ],
        compiler_params=pltpu.CompilerParams(dimension_semantics=("parallel",), vmem_limit_bytes=VMEM_LIMIT),
        name="mla_prep",
    )(x2, *rope, gpre, w_lat, gq, w_uq_t, gkv, w_k, w_vt)


def _mla_attn_kernel(qt_ref, k_ref, vt_ref, o_ref, *s_bufs, seq):
    n_blocks = seq // ATTN_Q
    n_chunks = seq // ATTN_K

    def q_block(hd, j):
        return qt_ref[0, hd, :, pl.ds(pl.multiple_of(j * ATTN_Q, ATTN_Q), ATTN_Q)]

    def read_probs(hd, c, m):
        return jnp.exp2((s_bufs[hd][c * ATTN_K:(c + 1) * ATTN_K, :] - m).astype(BF16))

    def write_scores(hd, c, q_t):
        s = _dot(k_ref[0, hd, c * ATTN_K:(c + 1) * ATTN_K, :], q_t)
        s_bufs[hd][c * ATTN_K:(c + 1) * ATTN_K, :] = s
        return jnp.max(s, axis=0, keepdims=True)

    def weighted_values(hd, c, m, acc):
        part = _dot(vt_ref[0, hd, :, c * ATTN_K:(c + 1) * ATTN_K], read_probs(hd, c, m))
        return part if acc is None else acc + part

    def emit(hd, j, acc):
        o_t = acc[:V_HEAD_DIM] * (1.0 / acc[V_HEAD_DIM:V_HEAD_DIM + 1])
        rows = pl.ds(pl.multiple_of(j * ATTN_Q, ATTN_Q), ATTN_Q)
        o_ref[0, rows, hd * V_HEAD_DIM:(hd + 1) * V_HEAD_DIM] = o_t.T.astype(o_ref.dtype)

    def col_max(parts):
        return functools.reduce(jnp.maximum, parts)

    def first_scores(hd):
        q_t = q_block(hd, 0)
        return col_max([write_scores(hd, c, q_t) for c in range(n_chunks)])

    def step(hd, j, carry):
        m_prev, acc_prev = carry
        emit(hd, jnp.maximum(j - 2, 0), acc_prev)
        q_t = q_block(hd, j)
        acc, maxes = None, []
        for c in range(n_chunks):
            acc = weighted_values(hd, c, m_prev, acc)
            maxes.append(write_scores(hd, c, q_t))
        return col_max(maxes), acc

    def all_heads(j, carries):
        return tuple(step(hd, j, carry) for hd, carry in enumerate(carries))

    init = tuple((first_scores(hd), jnp.ones((VT_ROWS, ATTN_Q), F32)) for hd in range(ATTN_HEADS))
    carries = lax.fori_loop(1, n_blocks, all_heads, init)
    for hd, (m, acc) in enumerate(carries):
        emit(hd, n_blocks - 2, acc)
        acc = None
        for c in range(n_chunks):
            acc = weighted_values(hd, c, m, acc)
        emit(hd, n_blocks - 1, acc)


def _mla_attn(q_t, k, vt):
    batch, heads, seq, _ = k.shape
    group = ATTN_HEADS
    return pl.pallas_call(
        functools.partial(_mla_attn_kernel, seq=seq),
        grid=(batch, heads // group),
        in_specs=[
            pl.BlockSpec((1, group, QK_PAD_DIM, seq), lambda b, g: (b, g, 0, 0)),
            pl.BlockSpec((1, group, seq, QK_PAD_DIM), lambda b, g: (b, g, 0, 0)),
            pl.BlockSpec((1, group, VT_ROWS, seq), lambda b, g: (b, g, 0, 0)),
        ],
        out_specs=pl.BlockSpec((1, seq, group * V_HEAD_DIM), lambda b, g: (b, 0, g)),
        out_shape=jax.ShapeDtypeStruct((batch, seq, MLA_WIDTH), BF16),
        scratch_shapes=[pltpu.VMEM((seq, ATTN_Q), F32)] * group,
        compiler_params=pltpu.CompilerParams(
            dimension_semantics=("parallel", "parallel"), vmem_limit_bytes=VMEM_LIMIT),
        name="mla_attn",
    )(q_t, k, vt)


def _mem_kv_kernel(mem_ref, g_ref, w_mk_ref, w_mv_ref, mk_ref, mv_ref):
    mem_n = _rms(mem_ref[0], g_ref[0]).astype(BF16)
    mk_ref[0, 0] = _dot(mem_n, w_mk_ref[0]).astype(BF16)
    mv_ref[0, 0] = _dot(mem_n, w_mv_ref[0]).astype(BF16)


def _mem_kv(mem, g, w_mk, w_mv):
    batch = mem.shape[0]
    out = jax.ShapeDtypeStruct((DEPTH, batch, MEM_TOKENS, MEM_WIDTH), BF16)
    return pl.pallas_call(
        _mem_kv_kernel,
        grid=(DEPTH, batch),
        in_specs=[
            pl.BlockSpec((1, MEM_TOKENS, D_MODEL), lambda l, b: (b, 0, 0)),
            pl.BlockSpec((1, 1, D_MODEL), lambda l, b: (l, 0, 0)),
            pl.BlockSpec((1, D_MODEL, MEM_WIDTH), lambda l, b: (l, 0, 0)),
            pl.BlockSpec((1, D_MODEL, MEM_WIDTH), lambda l, b: (l, 0, 0)),
        ],
        out_specs=[
            pl.BlockSpec((1, 1, MEM_TOKENS, MEM_WIDTH), lambda l, b: (l, b, 0, 0)),
            pl.BlockSpec((1, 1, MEM_TOKENS, MEM_WIDTH), lambda l, b: (l, b, 0, 0)),
        ],
        out_shape=[out, out],
        compiler_params=pltpu.CompilerParams(dimension_semantics=("parallel", "parallel")),
        name="mem_kv",
    )(mem, g, w_mk, w_mv)


def _mix_kernel(h_ref, hp_ref, hn_ref, a_ref, mk_ref, mv_ref, w_ref, cw_ref, y_ref, *, seq):
    c_gc, c_qm = CONV_WIDTH, 3 * CONV_WIDTH
    rows = h_ref.shape[0]
    i = pl.program_id(0)
    keep_prev = ((i * rows) % seq != 0).astype(BF16)
    keep_next = (((i + 1) * rows) % seq != 0).astype(BF16)
    h_ext = jnp.concatenate([hp_ref[...] * keep_prev, h_ref[...], hn_ref[...] * keep_next], axis=0)

    zc = _dot_nt(h_ext, w_ref[c_gc:c_qm, :])
    u = zc[:, :CONV_WIDTH] * zc[:, CONV_WIDTH:]
    ext = rows + 2 * HALO
    body = slice(HALO, HALO + rows)
    conv = (pltpu.roll(u, 1, 0)[body] * cw_ref[0:1, :] + u[body] * cw_ref[1:2, :]
            + pltpu.roll(u, ext - 1, 0)[body] * cw_ref[2:3, :])

    h = h_ref[...]
    c_gate = c_qm + MEM_WIDTH

    def gated(val, lo, hi):
        gate = _dot_nt(h, w_ref[c_gate + lo:c_gate + hi, :])
        y_ref[:, lo:hi] = (val * (gate * (1.0 / (1.0 + jnp.exp(-gate))))).astype(BF16)

    gated(_dot_nt(h, w_ref[:c_gc, :]) * conv, MLA_WIDTH, MLA_WIDTH + CONV_WIDTH)

    qm = _dot_nt(h, w_ref[c_qm:c_gate, :]) * (MEM_HEAD_DIM ** -0.5)
    m_out = []
    for hd in range(MEM_HEADS):
        lo, hi = hd * MEM_HEAD_DIM, (hd + 1) * MEM_HEAD_DIM
        s = _dot_nt(qm[:, lo:hi].astype(BF16), mk_ref[0, :, lo:hi])
        p = jnp.exp(s - jnp.max(s, axis=-1, keepdims=True))
        m_out.append(_dot(p.astype(BF16), mv_ref[0, :, lo:hi]) * (1.0 / jnp.sum(p, axis=-1, keepdims=True)))
    gated(jnp.concatenate(m_out, axis=1), MLA_WIDTH + CONV_WIDTH, MIX_WIDTH)

    for lo in range(0, MLA_WIDTH, GATE_COLS):
        gated(a_ref[:, lo:lo + GATE_COLS].astype(F32), lo, lo + GATE_COLS)


def _mix(h2, a2, mk, mv, layer, w_rest, cw, batch, seq):
    rows = MIX_ROWS
    tokens = batch * seq
    per_seq = seq // rows
    tiles = rows // HALO
    last_tile = tokens // HALO - 1
    single = pl.Buffered(1)
    return pl.pallas_call(
        functools.partial(_mix_kernel, seq=seq),
        grid=(tokens // rows,),
        in_specs=[
            pl.BlockSpec((rows, D_MODEL), lambda i: (i, 0)),
            pl.BlockSpec((HALO, D_MODEL), lambda i: (jnp.maximum(i * tiles - 1, 0), 0)),
            pl.BlockSpec((HALO, D_MODEL), lambda i: (jnp.minimum((i + 1) * tiles, last_tile), 0)),
            pl.BlockSpec((rows, MLA_WIDTH), lambda i: (i, 0)),
            pl.BlockSpec((None, 1, MEM_TOKENS, MEM_WIDTH), lambda i: (layer, i // per_seq, 0, 0)),
            pl.BlockSpec((None, 1, MEM_TOKENS, MEM_WIDTH), lambda i: (layer, i // per_seq, 0, 0)),
            _layer_spec(w_rest, layer, pipeline_mode=single),
            _layer_spec(cw, layer),
        ],
        out_specs=pl.BlockSpec((rows, MIX_WIDTH), lambda i: (i, 0)),
        out_shape=jax.ShapeDtypeStruct((tokens, MIX_WIDTH), BF16),
        compiler_params=pltpu.CompilerParams(dimension_semantics=("parallel",), vmem_limit_bytes=VMEM_LIMIT),
        name="mix",
    )(h2, h2, h2, a2, mk, mv, w_rest, cw)


def _out_proj_kernel(y_ref, w_o_ref, x_ref, g_ref, o_ref):
    o = _dot(y_ref[...], w_o_ref[...])
    o_ref[...] = x_ref[...] + _rms(o, g_ref[...])


def _out_proj(y2, w_o, x2, g, layer):
    rows = OUT_ROWS
    tokens = x2.shape[0]
    return pl.pallas_call(
        _out_proj_kernel,
        grid=(tokens // rows,),
        in_specs=[
            pl.BlockSpec((rows, MIX_WIDTH), lambda i: (i, 0)),
            _layer_spec(w_o, layer, pipeline_mode=pl.Buffered(1)),
            pl.BlockSpec((rows, D_MODEL), lambda i: (i, 0)),
            _layer_spec(g, layer),
        ],
        out_specs=pl.BlockSpec((rows, D_MODEL), lambda i: (i, 0)),
        out_shape=jax.ShapeDtypeStruct(x2.shape, F32),
        compiler_params=pltpu.CompilerParams(dimension_semantics=("parallel",), vmem_limit_bytes=VMEM_LIMIT),
        name="out_proj",
    )(y2, w_o, x2, g)


def _rope_inv_freq_col():
    inv_freq = 1.0 / (ROPE_THETA ** (jnp.arange(0, QK_ROPE_DIM, 2, dtype=F32) / QK_ROPE_DIM))
    return inv_freq[:, None]


def kernel(x, mem, positions, pre_norm_g, w_in, q_norm_g, w_uq, kv_norm_g, w_ukv, conv_w, mem_norm_g, w_mk, w_mv,
           w_o, post_norm_g):
    batch, seq, _ = x.shape
    tokens = batch * seq
    assert seq % PREP_ROWS == 0 and seq % MIX_ROWS == 0 and seq % ATTN_Q == 0 and seq % ATTN_K == 0
    assert tokens % OUT_ROWS == 0 and MLA_HEADS % ATTN_HEADS == 0

    c1 = Q_LORA_RANK + KV_LORA_RANK + QK_ROPE_DIM
    w_in_t = jnp.swapaxes(w_in, 1, 2)
    w_lat = jnp.pad(w_in_t[:, :c1, :], ((0, 0), (0, LANES - QK_ROPE_DIM), (0, 0))).astype(BF16)
    w_rest = w_in_t[:, c1:, :].astype(BF16)

    w_uq_h = w_uq.reshape(DEPTH, Q_LORA_RANK, MLA_HEADS, QK_HEAD_DIM)
    w_uq_nope = w_uq_h[..., :QK_NOPE_DIM].reshape(DEPTH, Q_LORA_RANK, MLA_WIDTH)
    w_uq_rope = w_uq_h[..., QK_NOPE_DIM:].reshape(DEPTH, Q_LORA_RANK, MLA_HEADS * QK_ROPE_DIM)
    w_uq_t = jnp.swapaxes(jnp.concatenate([w_uq_nope, w_uq_rope], axis=-1), 1, 2).astype(BF16)

    w_ukv_h = w_ukv.reshape(DEPTH, KV_LORA_RANK, MLA_HEADS, QK_NOPE_DIM + V_HEAD_DIM)
    w_k = w_ukv_h[..., :QK_NOPE_DIM].reshape(DEPTH, KV_LORA_RANK, MLA_WIDTH).astype(BF16)
    w_vt = jnp.swapaxes(w_ukv_h[..., QK_NOPE_DIM:].reshape(DEPTH, KV_LORA_RANK, MLA_WIDTH), 1, 2).astype(BF16)

    w_o_b = w_o.astype(BF16)
    gpre, gq, gkv, gpost = (g[:, None, :] for g in (pre_norm_g, q_norm_g, kv_norm_g, post_norm_g))
    rope = _rope_tables(positions.reshape(1, tokens), _rope_inv_freq_col())
    mk, mv = _mem_kv(mem, mem_norm_g[:, None, :], w_mk.astype(BF16), w_mv.astype(BF16))

    x2 = x.reshape(tokens, D_MODEL)
    for l in range(DEPTH):
        q_t, k, vt, h2 = _mla_prep(x2, rope, l, gpre, w_lat, gq, w_uq_t, gkv, w_k, w_vt, batch, seq)
        a = _mla_attn(q_t, k, vt)
        y = _mix(h2, a.reshape(tokens, MLA_WIDTH), mk, mv, l, w_rest, conv_w, batch, seq)
        x2 = _out_proj(y, w_o_b, x2, gpost, l)
    return x2.reshape(batch, seq, D_MODEL)
```

```python
import functools
import math

import jax
import jax.numpy as jnp
from jax import lax
from jax.experimental import pallas as pl
from jax.experimental.pallas import tpu as pltpu

D_MODEL = 2048
DEPTH = 4
MEM_TOKENS = 256
EPS = 1e-6
ROPE_THETA = 10000.0
MLA_HEADS = 8
QK_NOPE_DIM = 128
QK_ROPE_DIM = 64
QK_HEAD_DIM = QK_NOPE_DIM + QK_ROPE_DIM
V_HEAD_DIM = 128
Q_LORA_RANK = 512
KV_LORA_RANK = 256
MLA_WIDTH = MLA_HEADS * V_HEAD_DIM
CONV_WIDTH = 512
MEM_HEADS = 4
MEM_HEAD_DIM = 128
MEM_WIDTH = MEM_HEADS * MEM_HEAD_DIM
MIX_WIDTH = MLA_WIDTH + CONV_WIDTH + MEM_WIDTH

LANES = 128
SUBLANES = 8
QK_PAD_DIM = QK_NOPE_DIM + LANES
ONES_ROWS = 2 * SUBLANES
HALO = 2 * SUBLANES
VT_ROWS = V_HEAD_DIM + ONES_ROWS

BF16 = jnp.bfloat16
F32 = jnp.float32

PREP_ROWS = 1024
ROPE_COLS = 2048
MIX_ROWS = 1024
GATE_COLS = 512
OUT_ROWS = 1024
ATTN_Q = 256
ATTN_K = 512
ATTN_HEADS = 2
VMEM_LIMIT = 60 * 1024 * 1024

_NT = (((1,), (1,)), ((), ()))


def _rms(x, g):
    return x * lax.rsqrt(jnp.mean(x * x, axis=-1, keepdims=True) + EPS) * g


def _dot(a, b):
    return jnp.dot(a, b, preferred_element_type=F32)


def _dot_nt(a, b):
    return lax.dot_general(a, b, _NT, preferred_element_type=F32)


def _layer_spec(stacked, layer, **kwargs):
    return pl.BlockSpec((None,) + stacked.shape[1:], lambda i: (layer, 0, 0), **kwargs)


def _rope_tables_kernel(pos_ref, invf_ref, cos_t_ref, sin_t_ref):
    ang = invf_ref[...] * pos_ref[...].astype(F32)
    cos_t_ref[...] = jnp.cos(ang)
    sin_t_ref[...] = jnp.sin(ang)


def _rope_tables(pos_row, invf_col):
    tokens = pos_row.shape[1]
    cols = ROPE_COLS
    half = QK_ROPE_DIM // 2
    table = jax.ShapeDtypeStruct((half, tokens), F32)
    return pl.pallas_call(
        _rope_tables_kernel,
        grid=(tokens // cols,),
        in_specs=[pl.BlockSpec((1, cols), lambda i: (0, i)), pl.BlockSpec(invf_col.shape, lambda i: (0, 0))],
        out_specs=[pl.BlockSpec((half, cols), lambda i: (0, i)), pl.BlockSpec((half, cols), lambda i: (0, i))],
        out_shape=[table, table],
        compiler_params=pltpu.CompilerParams(dimension_semantics=("parallel",)),
        name="rope_tables",
    )(pos_row, invf_col)


def _mla_prep_kernel(x_ref, cos_t_ref, sin_t_ref, gpre_ref, w_lat_ref, gq_ref, w_uqt_ref, gkv_ref,
                     w_k_ref, w_vt_ref, qt_ref, k_ref, vt_ref, h_ref):
    h = _rms(x_ref[...], gpre_ref[...]).astype(BF16)
    h_ref[...] = h
    z = _dot_nt(h, w_lat_ref[...])
    qn = _rms(z[:, :Q_LORA_RANK], gq_ref[...]).astype(BF16)
    kvn = _rms(z[:, Q_LORA_RANK:Q_LORA_RANK + KV_LORA_RANK], gkv_ref[...]).astype(BF16)
    kpe_t = z[:, Q_LORA_RANK + KV_LORA_RANK:].T

    half = QK_ROPE_DIM // 2
    cos_t = cos_t_ref[...]
    sin_t = sin_t_ref[...]

    def rope_t(t):
        x1, x2 = t[:half], t[half:]
        return x1 * cos_t - x2 * sin_t, x2 * cos_t + x1 * sin_t

    rows = z.shape[0]
    k1, k2 = rope_t(kpe_t[:QK_ROPE_DIM])
    kpe_r = jnp.concatenate([k1, k2, jnp.zeros((LANES - QK_ROPE_DIM, rows), F32)], axis=0).T.astype(BF16)

    q_scale = (QK_HEAD_DIM ** -0.5) * math.log2(math.e)
    q_t = _dot_nt(w_uqt_ref[...], qn) * q_scale
    kn = _dot(kvn, w_k_ref[...])
    vt = _dot_nt(w_vt_ref[...], kvn)
    for hd in range(MLA_HEADS):
        lo, hi = hd * LANES, (hd + 1) * LANES
        r_lo = MLA_WIDTH + hd * QK_ROPE_DIM
        q1, q2 = rope_t(q_t[r_lo:r_lo + QK_ROPE_DIM])
        qt_ref[0, hd, :QK_NOPE_DIM, :] = q_t[lo:hi].astype(BF16)
        qt_ref[0, hd, QK_NOPE_DIM:QK_NOPE_DIM + half, :] = q1.astype(BF16)
        qt_ref[0, hd, QK_NOPE_DIM + half:QK_HEAD_DIM, :] = q2.astype(BF16)
        qt_ref[0, hd, QK_HEAD_DIM:, :] = jnp.zeros((QK_PAD_DIM - QK_HEAD_DIM, rows), BF16)
        k_ref[0, hd, :, :QK_NOPE_DIM] = kn[:, lo:hi].astype(BF16)
        k_ref[0, hd, :, QK_NOPE_DIM:] = kpe_r
        vt_ref[0, hd, :V_HEAD_DIM, :] = vt[lo:hi, :].astype(BF16)
        vt_ref[0, hd, V_HEAD_DIM:, :] = jnp.ones((ONES_ROWS, rows), BF16)


def _mla_prep(x2, rope, layer, gpre, w_lat, gq, w_uq_t, gkv, w_k, w_vt, batch, seq):
    rows = PREP_ROWS
    per_seq = seq // rows
    half = QK_ROPE_DIM // 2
    head_rows = lambda i: (i // per_seq, 0, i % per_seq, 0)
    head_cols = lambda i: (i // per_seq, 0, 0, i % per_seq)
    return pl.pallas_call(
        _mla_prep_kernel,
        grid=(batch * seq // rows,),
        in_specs=[
            pl.BlockSpec((rows, D_MODEL), lambda i: (i, 0)),
            pl.BlockSpec((half, rows), lambda i: (0, i)),
            pl.BlockSpec((half, rows), lambda i: (0, i)),
            _layer_spec(gpre, layer),
            _layer_spec(w_lat, layer),
            _layer_spec(gq, layer),
            _layer_spec(w_uq_t, layer),
            _layer_spec(gkv, layer),
            _layer_spec(w_k, layer),
            _layer_spec(w_vt, layer),
        ],
        out_specs=[
            pl.BlockSpec((1, MLA_HEADS, QK_PAD_DIM, rows), head_cols),
            pl.BlockSpec((1, MLA_HEADS, rows, QK_PAD_DIM), head_rows),
            pl.BlockSpec((1, MLA_HEADS, VT_ROWS, rows), head_cols),
            pl.BlockSpec((rows, D_MODEL), lambda i: (i, 0)),
        ],
        out_shape=[
            jax.ShapeDtypeStruct((batch, MLA_HEADS, QK_PAD_DIM, seq), BF16),
            jax.ShapeDtypeStruct((batch, MLA_HEADS, seq, QK_PAD_DIM), BF16),
            jax.ShapeDtypeStruct((batch, MLA_HEADS, VT_ROWS, seq), BF16),
            jax.ShapeDtypeStruct((batch * seq, D_MODEL), BF16),
        ],
        compiler_params=pltpu.CompilerParams(dimension_semantics=("parallel",), vmem_limit_bytes=VMEM_LIMIT),
        name="mla_prep",
    )(x2, *rope, gpre, w_lat, gq, w_uq_t, gkv, w_k, w_vt)


def _mla_attn_kernel(qt_ref, k_ref, vt_ref, o_ref, *s_bufs, seq):
    n_blocks = seq // ATTN_Q
    n_chunks = seq // ATTN_K

    def q_block(hd, j):
        return qt_ref[0, hd, :, pl.ds(pl.multiple_of(j * ATTN_Q, ATTN_Q), ATTN_Q)]

    def read_probs(hd, c, m):
        return jnp.exp2((s_bufs[hd][c * ATTN_K:(c + 1) * ATTN_K, :] - m).astype(BF16))

    def write_scores(hd, c, q_t):
        s = _dot(k_ref[0, hd, c * ATTN_K:(c + 1) * ATTN_K, :], q_t)
        s_bufs[hd][c * ATTN_K:(c + 1) * ATTN_K, :] = s
        return jnp.max(s, axis=0, keepdims=True)

    def weighted_values(hd, c, m, acc):
        part = _dot(vt_ref[0, hd, :, c * ATTN_K:(c + 1) * ATTN_K], read_probs(hd, c, m))
        return part if acc is None else acc + part

    def emit(hd, j, acc):
        o_t = acc[:V_HEAD_DIM] * (1.0 / acc[V_HEAD_DIM:V_HEAD_DIM + 1])
        rows = pl.ds(pl.multiple_of(j * ATTN_Q, ATTN_Q), ATTN_Q)
        o_ref[0, rows, hd * V_HEAD_DIM:(hd + 1) * V_HEAD_DIM] = o_t.T.astype(o_ref.dtype)

    def col_max(parts):
        return functools.reduce(jnp.maximum, parts)

    def first_scores(hd):
        q_t = q_block(hd, 0)
        return col_max([write_scores(hd, c, q_t) for c in range(n_chunks)])

    def step(hd, j, carry):
        m_prev, acc_prev = carry
        emit(hd, jnp.maximum(j - 2, 0), acc_prev)
        q_t = q_block(hd, j)
        acc, maxes = None, []
        for c in range(n_chunks):
            acc = weighted_values(hd, c, m_prev, acc)
            maxes.append(write_scores(hd, c, q_t))
        return col_max(maxes), acc

    def all_heads(j, carries):
        return tuple(step(hd, j, carry) for hd, carry in enumerate(carries))

    init = tuple((first_scores(hd), jnp.ones((VT_ROWS, ATTN_Q), F32)) for hd in range(ATTN_HEADS))
    carries = lax.fori_loop(1, n_blocks, all_heads, init)
    for hd, (m, acc) in enumerate(carries):
        emit(hd, n_blocks - 2, acc)
        acc = None
        for c in range(n_chunks):
            acc = weighted_values(hd, c, m, acc)
        emit(hd, n_blocks - 1, acc)


def _mla_attn(q_t, k, vt):
    batch, heads, seq, _ = k.shape
    group = ATTN_HEADS
    return pl.pallas_call(
        functools.partial(_mla_attn_kernel, seq=seq),
        grid=(batch, heads // group),
        in_specs=[
            pl.BlockSpec((1, group, QK_PAD_DIM, seq), lambda b, g: (b, g, 0, 0)),
            pl.BlockSpec((1, group, seq, QK_PAD_DIM), lambda b, g: (b, g, 0, 0)),
            pl.BlockSpec((1, group, VT_ROWS, seq), lambda b, g: (b, g, 0, 0)),
        ],
        out_specs=pl.BlockSpec((1, seq, group * V_HEAD_DIM), lambda b, g: (b, 0, g)),
        out_shape=jax.ShapeDtypeStruct((batch, seq, MLA_WIDTH), BF16),
        scratch_shapes=[pltpu.VMEM((seq, ATTN_Q), F32)] * group,
        compiler_params=pltpu.CompilerParams(
            dimension_semantics=("parallel", "parallel"), vmem_limit_bytes=VMEM_LIMIT),
        name="mla_attn",
    )(q_t, k, vt)


def _mem_kv_kernel(mem_ref, g_ref, w_mk_ref, w_mv_ref, mk_ref, mv_ref):
    mem_n = _rms(mem_ref[0], g_ref[0]).astype(BF16)
    mk_ref[0, 0] = _dot(mem_n, w_mk_ref[0]).astype(BF16)
    mv_ref[0, 0] = _dot(mem_n, w_mv_ref[0]).astype(BF16)


def _mem_kv(mem, g, w_mk, w_mv):
    batch = mem.shape[0]
    out = jax.ShapeDtypeStruct((DEPTH, batch, MEM_TOKENS, MEM_WIDTH), BF16)
    return pl.pallas_call(
        _mem_kv_kernel,
        grid=(DEPTH, batch),
        in_specs=[
            pl.BlockSpec((1, MEM_TOKENS, D_MODEL), lambda l, b: (b, 0, 0)),
            pl.BlockSpec((1, 1, D_MODEL), lambda l, b: (l, 0, 0)),
            pl.BlockSpec((1, D_MODEL, MEM_WIDTH), lambda l, b: (l, 0, 0)),
            pl.BlockSpec((1, D_MODEL, MEM_WIDTH), lambda l, b: (l, 0, 0)),
        ],
        out_specs=[
            pl.BlockSpec((1, 1, MEM_TOKENS, MEM_WIDTH), lambda l, b: (l, b, 0, 0)),
            pl.BlockSpec((1, 1, MEM_TOKENS, MEM_WIDTH), lambda l, b: (l, b, 0, 0)),
        ],
        out_shape=[out, out],
        compiler_params=pltpu.CompilerParams(dimension_semantics=("parallel", "parallel")),
        name="mem_kv",
    )(mem, g, w_mk, w_mv)


def _mix_kernel(h_ref, hp_ref, hn_ref, a_ref, mk_ref, mv_ref, w_ref, cw_ref, y_ref, *, seq):
    c_gc, c_qm = CONV_WIDTH, 3 * CONV_WIDTH
    rows = h_ref.shape[0]
    i = pl.program_id(0)
    keep_prev = ((i * rows) % seq != 0).astype(BF16)
    keep_next = (((i + 1) * rows) % seq != 0).astype(BF16)
    h_ext = jnp.concatenate([hp_ref[...] * keep_prev, h_ref[...], hn_ref[...] * keep_next], axis=0)

    zc = _dot_nt(h_ext, w_ref[c_gc:c_qm, :])
    u = zc[:, :CONV_WIDTH] * zc[:, CONV_WIDTH:]
    ext = rows + 2 * HALO
    body = slice(HALO, HALO + rows)
    conv = (pltpu.roll(u, 1, 0)[body] * cw_ref[0:1, :] + u[body] * cw_ref[1:2, :]
            + pltpu.roll(u, ext - 1, 0)[body] * cw_ref[2:3, :])

    h = h_ref[...]
    c_gate = c_qm + MEM_WIDTH

    def gated(val, lo, hi):
        gate = _dot_nt(h, w_ref[c_gate + lo:c_gate + hi, :])
        y_ref[:, lo:hi] = (val * (gate * (1.0 / (1.0 + jnp.exp(-gate))))).astype(BF16)

    gated(_dot_nt(h, w_ref[:c_gc, :]) * conv, MLA_WIDTH, MLA_WIDTH + CONV_WIDTH)

    qm = _dot_nt(h, w_ref[c_qm:c_gate, :]) * (MEM_HEAD_DIM ** -0.5)
    m_out = []
    for hd in range(MEM_HEADS):
        lo, hi = hd * MEM_HEAD_DIM, (hd + 1) * MEM_HEAD_DIM
        s = _dot_nt(qm[:, lo:hi].astype(BF16), mk_ref[0, :, lo:hi])
        p = jnp.exp(s - jnp.max(s, axis=-1, keepdims=True))
        m_out.append(_dot(p.astype(BF16), mv_ref[0, :, lo:hi]) * (1.0 / jnp.sum(p, axis=-1, keepdims=True)))
    gated(jnp.concatenate(m_out, axis=1), MLA_WIDTH + CONV_WIDTH, MIX_WIDTH)

    for lo in range(0, MLA_WIDTH, GATE_COLS):
        gated(a_ref[:, lo:lo + GATE_COLS].astype(F32), lo, lo + GATE_COLS)


def _mix(h2, a2, mk, mv, layer, w_rest, cw, batch, seq):
    rows = MIX_ROWS
    tokens = batch * seq
    per_seq = seq // rows
    tiles = rows // HALO
    last_tile = tokens // HALO - 1
    single = pl.Buffered(1)
    return pl.pallas_call(
        functools.partial(_mix_kernel, seq=seq),
        grid=(tokens // rows,),
        in_specs=[
            pl.BlockSpec((rows, D_MODEL), lambda i: (i, 0)),
            pl.BlockSpec((HALO, D_MODEL), lambda i: (jnp.maximum(i * tiles - 1, 0), 0)),
            pl.BlockSpec((HALO, D_MODEL), lambda i: (jnp.minimum((i + 1) * tiles, last_tile), 0)),
            pl.BlockSpec((rows, MLA_WIDTH), lambda i: (i, 0)),
            pl.BlockSpec((None, 1, MEM_TOKENS, MEM_WIDTH), lambda i: (layer, i // per_seq, 0, 0)),
            pl.BlockSpec((None, 1, MEM_TOKENS, MEM_WIDTH), lambda i: (layer, i // per_seq, 0, 0)),
            _layer_spec(w_rest, layer, pipeline_mode=single),
            _layer_spec(cw, layer),
        ],
        out_specs=pl.BlockSpec((rows, MIX_WIDTH), lambda i: (i, 0)),
        out_shape=jax.ShapeDtypeStruct((tokens, MIX_WIDTH), BF16),
        compiler_params=pltpu.CompilerParams(dimension_semantics=("parallel",), vmem_limit_bytes=VMEM_LIMIT),
        name="mix",
    )(h2, h2, h2, a2, mk, mv, w_rest, cw)


def _out_proj_kernel(y_ref, w_o_ref, x_ref, g_ref, o_ref):
    o = _dot(y_ref[...], w_o_ref[...])
    o_ref[...] = x_ref[...] + _rms(o, g_ref[...])


def _out_proj(y2, w_o, x2, g, layer):
    rows = OUT_ROWS
    tokens = x2.shape[0]
    return pl.pallas_call(
        _out_proj_kernel,
        grid=(tokens // rows,),
        in_specs=[
            pl.BlockSpec((rows, MIX_WIDTH), lambda i: (i, 0)),
            _layer_spec(w_o, layer, pipeline_mode=pl.Buffered(1)),
            pl.BlockSpec((rows, D_MODEL), lambda i: (i, 0)),
            _layer_spec(g, layer),
        ],
        out_specs=pl.BlockSpec((rows, D_MODEL), lambda i: (i, 0)),
        out_shape=jax.ShapeDtypeStruct(x2.shape, F32),
        compiler_params=pltpu.CompilerParams(dimension_semantics=("parallel",), vmem_limit_bytes=VMEM_LIMIT),
        name="out_proj",
    )(y2, w_o, x2, g)


def _rope_inv_freq_col():
    inv_freq = 1.0 / (ROPE_THETA ** (jnp.arange(0, QK_ROPE_DIM, 2, dtype=F32) / QK_ROPE_DIM))
    return inv_freq[:, None]


def kernel(x, mem, positions, pre_norm_g, w_in, q_norm_g, w_uq, kv_norm_g, w_ukv, conv_w, mem_norm_g, w_mk, w_mv,
           w_o, post_norm_g):
    batch, seq, _ = x.shape
    tokens = batch * seq
    assert seq % PREP_ROWS == 0 and seq % MIX_ROWS == 0 and seq % ATTN_Q == 0 and seq % ATTN_K == 0
    assert tokens % OUT_ROWS == 0 and MLA_HEADS % ATTN_HEADS == 0

    c1 = Q_LORA_RANK + KV_LORA_RANK + QK_ROPE_DIM
    w_in_t = jnp.swapaxes(w_in, 1, 2)
    w_lat = jnp.pad(w_in_t[:, :c1, :], ((0, 0), (0, LANES - QK_ROPE_DIM), (0, 0))).astype(BF16)
    w_rest = w_in_t[:, c1:, :].astype(BF16)

    w_uq_h = w_uq.reshape(DEPTH, Q_LORA_RANK, MLA_HEADS, QK_HEAD_DIM)
    w_uq_nope = w_uq_h[..., :QK_NOPE_DIM].reshape(DEPTH, Q_LORA_RANK, MLA_WIDTH)
    w_uq_rope = w_uq_h[..., QK_NOPE_DIM:].reshape(DEPTH, Q_LORA_RANK, MLA_HEADS * QK_ROPE_DIM)
    w_uq_t = jnp.swapaxes(jnp.concatenate([w_uq_nope, w_uq_rope], axis=-1), 1, 2).astype(BF16)

    w_ukv_h = w_ukv.reshape(DEPTH, KV_LORA_RANK, MLA_HEADS, QK_NOPE_DIM + V_HEAD_DIM)
    w_k = w_ukv_h[..., :QK_NOPE_DIM].reshape(DEPTH, KV_LORA_RANK, MLA_WIDTH).astype(BF16)
    w_vt = jnp.swapaxes(w_ukv_h[..., QK_NOPE_DIM:].reshape(DEPTH, KV_LORA_RANK, MLA_WIDTH), 1, 2).astype(BF16)

    w_o_b = w_o.astype(BF16)
    gpre, gq, gkv, gpost = (g[:, None, :] for g in (pre_norm_g, q_norm_g, kv_norm_g, post_norm_g))
    rope = _rope_tables(positions.reshape(1, tokens), _rope_inv_freq_col())
    mk, mv = _mem_kv(mem, mem_norm_g[:, None, :], w_mk.astype(BF16), w_mv.astype(BF16))

    x2 = x.reshape(tokens, D_MODEL)
    for l in range(DEPTH):
        q_t, k, vt, h2 = _mla_prep(x2, rope, l, gpre, w_lat, gq, w_uq_t, gkv, w_k, w_vt, batch, seq)
        a = _mla_attn(q_t, k, vt)
        y = _mix(h2, a.reshape(tokens, MLA_WIDTH), mk, mv, l, w_rest, conv_w, batch, seq)
        x2 = _out_proj(y, w_o_b, x2, gpost, l)
    return x2.reshape(batch, seq, D_MODEL)
```

```python
import functools
import math

import jax
import jax.numpy as jnp
from jax import lax
from jax.experimental import pallas as pl
from jax.experimental.pallas import tpu as pltpu

D_MODEL = 2048
DEPTH = 4
MEM_TOKENS = 256
EPS = 1e-6
ROPE_THETA = 10000.0
MLA_HEADS = 8
QK_NOPE_DIM = 128
QK_ROPE_DIM = 64
QK_HEAD_DIM = QK_NOPE_DIM + QK_ROPE_DIM
V_HEAD_DIM = 128
Q_LORA_RANK = 512
KV_LORA_RANK = 256
MLA_WIDTH = MLA_HEADS * V_HEAD_DIM
CONV_WIDTH = 512
MEM_HEADS = 4
MEM_HEAD_DIM = 128
MEM_WIDTH = MEM_HEADS * MEM_HEAD_DIM
MIX_WIDTH = MLA_WIDTH + CONV_WIDTH + MEM_WIDTH

LANES = 128
SUBLANES = 8
QK_PAD_DIM = QK_NOPE_DIM + LANES
ONES_ROWS = 2 * SUBLANES
HALO = 2 * SUBLANES
VT_ROWS = V_HEAD_DIM + ONES_ROWS

BF16 = jnp.bfloat16
F32 = jnp.float32

PREP_ROWS = 1024
ROPE_COLS = 2048
MIX_ROWS = 1024
GATE_COLS = 512
OUT_ROWS = 1024
ATTN_Q = 256
ATTN_K = 512
ATTN_HEADS = 2
VMEM_LIMIT = 60 * 1024 * 1024

_NT = (((1,), (1,)), ((), ()))


def _rms(x, g):
    return x * lax.rsqrt(jnp.mean(x * x, axis=-1, keepdims=True) + EPS) * g


def _dot(a, b):
    return jnp.dot(a, b, preferred_element_type=F32)


def _dot_nt(a, b):
    return lax.dot_general(a, b, _NT, preferred_element_type=F32)


def _layer_spec(stacked, layer, **kwargs):
    return pl.BlockSpec((None,) + stacked.shape[1:], lambda i: (layer, 0, 0), **kwargs)


def _rope_tables_kernel(pos_ref, invf_ref, cos_t_ref, sin_t_ref):
    ang = invf_ref[...] * pos_ref[...].astype(F32)
    cos_t_ref[...] = jnp.cos(ang)
    sin_t_ref[...] = jnp.sin(ang)


def _rope_tables(pos_row, invf_col):
    tokens = pos_row.shape[1]
    cols = ROPE_COLS
    half = QK_ROPE_DIM // 2
    table = jax.ShapeDtypeStruct((half, tokens), F32)
    return pl.pallas_call(
        _rope_tables_kernel,
        grid=(tokens // cols,),
        in_specs=[pl.BlockSpec((1, cols), lambda i: (0, i)), pl.BlockSpec(invf_col.shape, lambda i: (0, 0))],
        out_specs=[pl.BlockSpec((half, cols), lambda i: (0, i)), pl.BlockSpec((half, cols), lambda i: (0, i))],
        out_shape=[table, table],
        compiler_params=pltpu.CompilerParams(dimension_semantics=("parallel",)),
        name="rope_tables",
    )(pos_row, invf_col)


def _mla_prep_kernel(x_ref, cos_t_ref, sin_t_ref, gpre_ref, w_lat_ref, gq_ref, w_uqt_ref, gkv_ref,
                     w_k_ref, w_vt_ref, qt_ref, k_ref, vt_ref, h_ref):
    h = _rms(x_ref[...], gpre_ref[...]).astype(BF16)
    h_ref[...] = h
    z = _dot_nt(h, w_lat_ref[...])
    qn = _rms(z[:, :Q_LORA_RANK], gq_ref[...]).astype(BF16)
    kvn = _rms(z[:, Q_LORA_RANK:Q_LORA_RANK + KV_LORA_RANK], gkv_ref[...]).astype(BF16)
    kpe_t = z[:, Q_LORA_RANK + KV_LORA_RANK:].T

    half = QK_ROPE_DIM // 2
    cos_t = cos_t_ref[...]
    sin_t = sin_t_ref[...]

    def rope_t(t):
        x1, x2 = t[:half], t[half:]
        return x1 * cos_t - x2 * sin_t, x2 * cos_t + x1 * sin_t

    rows = z.shape[0]
    k1, k2 = rope_t(kpe_t[:QK_ROPE_DIM])
    kpe_r = jnp.concatenate([k1, k2, jnp.zeros((LANES - QK_ROPE_DIM, rows), F32)], axis=0).T.astype(BF16)

    q_scale = (QK_HEAD_DIM ** -0.5) * math.log2(math.e)
    q_t = _dot_nt(w_uqt_ref[...], qn) * q_scale
    kn = _dot(kvn, w_k_ref[...])
    vt = _dot_nt(w_vt_ref[...], kvn)
    for hd in range(MLA_HEADS):
        lo, hi = hd * LANES, (hd + 1) * LANES
        r_lo = MLA_WIDTH + hd * QK_ROPE_DIM
        q1, q2 = rope_t(q_t[r_lo:r_lo + QK_ROPE_DIM])
        qt_ref[0, hd, :QK_NOPE_DIM, :] = q_t[lo:hi].astype(BF16)
        qt_ref[0, hd, QK_NOPE_DIM:QK_NOPE_DIM + half, :] = q1.astype(BF16)
        qt_ref[0, hd, QK_NOPE_DIM + half:QK_HEAD_DIM, :] = q2.astype(BF16)
        qt_ref[0, hd, QK_HEAD_DIM:, :] = jnp.zeros((QK_PAD_DIM - QK_HEAD_DIM, rows), BF16)
        k_ref[0, hd, :, :QK_NOPE_DIM] = kn[:, lo:hi].astype(BF16)
        k_ref[0, hd, :, QK_NOPE_DIM:] = kpe_r
        vt_ref[0, hd, :V_HEAD_DIM, :] = vt[lo:hi, :].astype(BF16)
        vt_ref[0, hd, V_HEAD_DIM:, :] = jnp.ones((ONES_ROWS, rows), BF16)


def _mla_prep(x2, rope, layer, gpre, w_lat, gq, w_uq_t, gkv, w_k, w_vt, batch, seq):
    rows = PREP_ROWS
    per_seq = seq // rows
    half = QK_ROPE_DIM // 2
    head_rows = lambda i: (i // per_seq, 0, i % per_seq, 0)
    head_cols = lambda i: (i // per_seq, 0, 0, i % per_seq)
    return pl.pallas_call(
        _mla_prep_kernel,
        grid=(batch * seq // rows,),
        in_specs=[
            pl.BlockSpec((rows, D_MODEL), lambda i: (i, 0)),
            pl.BlockSpec((half, rows), lambda i: (0, i)),
            pl.BlockSpec((half, rows), lambda i: (0, i)),
            _layer_spec(gpre, layer),
            _layer_spec(w_lat, layer),
            _layer_spec(gq, layer),
            _layer_spec(w_uq_t, layer),
            _layer_spec(gkv, layer),
            _layer_spec(w_k, layer),
            _layer_spec(w_vt, layer),
        ],
        out_specs=[
            pl.BlockSpec((1, MLA_HEADS, QK_PAD_DIM, rows), head_cols),
            pl.BlockSpec((1, MLA_HEADS, rows, QK_PAD_DIM), head_rows),
            pl.BlockSpec((1, MLA_HEADS, VT_ROWS, rows), head_cols),
            pl.BlockSpec((rows, D_MODEL), lambda i: (i, 0)),
        ],
        out_shape=[
            jax.ShapeDtypeStruct((batch, MLA_HEADS, QK_PAD_DIM, seq), BF16),
            jax.ShapeDtypeStruct((batch, MLA_HEADS, seq, QK_PAD_DIM), BF16),
            jax.ShapeDtypeStruct((batch, MLA_HEADS, VT_ROWS, seq), BF16),
            jax.ShapeDtypeStruct((batch * seq, D_MODEL), BF16),
        ],
        compiler_params=pltpu.CompilerParams(dimension_semantics=("parallel",), vmem_limit_bytes=VMEM_LIMIT),
        name="mla_prep",
    )(x2, *rope, gpre, w_lat, gq, w_uq_t, gkv, w_k, w_vt)


def _mla_attn_kernel(qt_ref, k_ref, vt_ref, o_ref, *s_bufs, seq):
    n_blocks = seq // ATTN_Q
    n_chunks = seq // ATTN_K

    def q_block(hd, j):
        return qt_ref[0, hd, :, pl.ds(pl.multiple_of(j * ATTN_Q, ATTN_Q), ATTN_Q)]

    def read_probs(hd, c, m):
        return jnp.exp2((s_bufs[hd][c * ATTN_K:(c + 1) * ATTN_K, :] - m).astype(BF16))

    def write_scores(hd, c, q_t):
        s = _dot(k_ref[0, hd, c * ATTN_K:(c + 1) * ATTN_K, :], q_t)
        s_bufs[hd][c * ATTN_K:(c + 1) * ATTN_K, :] = s
        return jnp.max(s, axis=0, keepdims=True)

    def weighted_values(hd, c, m, acc):
        part = _dot(vt_ref[0, hd, :, c * ATTN_K:(c + 1) * ATTN_K], read_probs(hd, c, m))
        return part if acc is None else acc + part

    def emit(hd, j, acc):
        o_t = acc[:V_HEAD_DIM] * (1.0 / acc[V_HEAD_DIM:V_HEAD_DIM + 1])
        rows = pl.ds(pl.multiple_of(j * ATTN_Q, ATTN_Q), ATTN_Q)
        o_ref[0, rows, hd * V_HEAD_DIM:(hd + 1) * V_HEAD_DIM] = o_t.T.astype(o_ref.dtype)

    def col_max(parts):
        return functools.reduce(jnp.maximum, parts)

    def first_scores(hd):
        q_t = q_block(hd, 0)
        return col_max([write_scores(hd, c, q_t) for c in range(n_chunks)])

    def step(hd, j, carry):
        m_prev, acc_prev = carry
        emit(hd, jnp.maximum(j - 2, 0), acc_prev)
        q_t = q_block(hd, j)
        acc, maxes = None, []
        for c in range(n_chunks):
            acc = weighted_values(hd, c, m_prev, acc)
            maxes.append(write_scores(hd, c, q_t))
        return col_max(maxes), acc

    def all_heads(j, carries):
        return tuple(step(hd, j, carry) for hd, carry in enumerate(carries))

    init = tuple((first_scores(hd), jnp.ones((VT_ROWS, ATTN_Q), F32)) for hd in range(ATTN_HEADS))
    carries = lax.fori_loop(1, n_blocks, all_heads, init)
    for hd, (m, acc) in enumerate(carries):
        emit(hd, n_blocks - 2, acc)
        acc = None
        for c in range(n_chunks):
            acc = weighted_values(hd, c, m, acc)
        emit(hd, n_blocks - 1, acc)


def _mla_attn(q_t, k, vt):
    batch, heads, seq, _ = k.shape
    group = ATTN_HEADS
    return pl.pallas_call(
        functools.partial(_mla_attn_kernel, seq=seq),
        grid=(batch, heads // group),
        in_specs=[
            pl.BlockSpec((1, group, QK_PAD_DIM, seq), lambda b, g: (b, g, 0, 0)),
            pl.BlockSpec((1, group, seq, QK_PAD_DIM), lambda b, g: (b, g, 0, 0)),
            pl.BlockSpec((1, group, VT_ROWS, seq), lambda b, g: (b, g, 0, 0)),
        ],
        out_specs=pl.BlockSpec((1, seq, group * V_HEAD_DIM), lambda b, g: (b, 0, g)),
        out_shape=jax.ShapeDtypeStruct((batch, seq, MLA_WIDTH), BF16),
        scratch_shapes=[pltpu.VMEM((seq, ATTN_Q), F32)] * group,
        compiler_params=pltpu.CompilerParams(
            dimension_semantics=("parallel", "parallel"), vmem_limit_bytes=VMEM_LIMIT),
        name="mla_attn",
    )(q_t, k, vt)


def _mem_kv_kernel(mem_ref, g_ref, w_mk_ref, w_mv_ref, mk_ref, mv_ref):
    mem_n = _rms(mem_ref[0], g_ref[0]).astype(BF16)
    mk_ref[0, 0] = _dot(mem_n, w_mk_ref[0]).astype(BF16)
    mv_ref[0, 0] = _dot(mem_n, w_mv_ref[0]).astype(BF16)


def _mem_kv(mem, g, w_mk, w_mv):
    batch = mem.shape[0]
    out = jax.ShapeDtypeStruct((DEPTH, batch, MEM_TOKENS, MEM_WIDTH), BF16)
    return pl.pallas_call(
        _mem_kv_kernel,
        grid=(DEPTH, batch),
        in_specs=[
            pl.BlockSpec((1, MEM_TOKENS, D_MODEL), lambda l, b: (b, 0, 0)),
            pl.BlockSpec((1, 1, D_MODEL), lambda l, b: (l, 0, 0)),
            pl.BlockSpec((1, D_MODEL, MEM_WIDTH), lambda l, b: (l, 0, 0)),
            pl.BlockSpec((1, D_MODEL, MEM_WIDTH), lambda l, b: (l, 0, 0)),
        ],
        out_specs=[
            pl.BlockSpec((1, 1, MEM_TOKENS, MEM_WIDTH), lambda l, b: (l, b, 0, 0)),
            pl.BlockSpec((1, 1, MEM_TOKENS, MEM_WIDTH), lambda l, b: (l, b, 0, 0)),
        ],
        out_shape=[out, out],
        compiler_params=pltpu.CompilerParams(dimension_semantics=("parallel", "parallel")),
        name="mem_kv",
    )(mem, g, w_mk, w_mv)


def _mix_kernel(h_ref, hp_ref, hn_ref, a_ref, mk_ref, mv_ref, w_ref, cw_ref, y_ref, *, seq):
    c_gc, c_qm = CONV_WIDTH, 3 * CONV_WIDTH
    rows = h_ref.shape[0]
    i = pl.program_id(0)
    keep_prev = ((i * rows) % seq != 0).astype(BF16)
    keep_next = (((i + 1) * rows) % seq != 0).astype(BF16)
    h_ext = jnp.concatenate([hp_ref[...] * keep_prev, h_ref[...], hn_ref[...] * keep_next], axis=0)

    zc = _dot_nt(h_ext, w_ref[c_gc:c_qm, :])
    u = zc[:, :CONV_WIDTH] * zc[:, CONV_WIDTH:]
    ext = rows + 2 * HALO
    body = slice(HALO, HALO + rows)
    conv = (pltpu.roll(u, 1, 0)[body] * cw_ref[0:1, :] + u[body] * cw_ref[1:2, :]
            + pltpu.roll(u, ext - 1, 0)[body] * cw_ref[2:3, :])

    h = h_ref[...]
    c_gate = c_qm + MEM_WIDTH

    def gated(val, lo, hi):
        gate = _dot_nt(h, w_ref[c_gate + lo:c_gate + hi, :])
        y_ref[:, lo:hi] = (val * (gate * (1.0 / (1.0 + jnp.exp(-gate))))).astype(BF16)

    gated(_dot_nt(h, w_ref[:c_gc, :]) * conv, MLA_WIDTH, MLA_WIDTH + CONV_WIDTH)

    qm = _dot_nt(h, w_ref[c_qm:c_gate, :]) * (MEM_HEAD_DIM ** -0.5)
    m_out = []
    for hd in range(MEM_HEADS):
        lo, hi = hd * MEM_HEAD_DIM, (hd + 1) * MEM_HEAD_DIM
        s = _dot_nt(qm[:, lo:hi].astype(BF16), mk_ref[0, :, lo:hi])
        p = jnp.exp(s - jnp.max(s, axis=-1, keepdims=True))
        m_out.append(_dot(p.astype(BF16), mv_ref[0, :, lo:hi]) * (1.0 / jnp.sum(p, axis=-1, keepdims=True)))
    gated(jnp.concatenate(m_out, axis=1), MLA_WIDTH + CONV_WIDTH, MIX_WIDTH)

    for lo in range(0, MLA_WIDTH, GATE_COLS):
        gated(a_ref[:, lo:lo + GATE_COLS].astype(F32), lo, lo + GATE_COLS)


def _mix(h2, a2, mk, mv, layer, w_rest, cw, batch, seq):
    rows = MIX_ROWS
    tokens = batch * seq
    per_seq = seq // rows
    tiles = rows // HALO
    last_tile = tokens // HALO - 1
    single = pl.Buffered(1)
    return pl.pallas_call(
        functools.partial(_mix_kernel, seq=seq),
        grid=(tokens // rows,),
        in_specs=[
            pl.BlockSpec((rows, D_MODEL), lambda i: (i, 0)),
            pl.BlockSpec((HALO, D_MODEL), lambda i: (jnp.maximum(i * tiles - 1, 0), 0)),
            pl.BlockSpec((HALO, D_MODEL), lambda i: (jnp.minimum((i + 1) * tiles, last_tile), 0)),
            pl.BlockSpec((rows, MLA_WIDTH), lambda i: (i, 0)),
            pl.BlockSpec((None, 1, MEM_TOKENS, MEM_WIDTH), lambda i: (layer, i // per_seq, 0, 0)),
            pl.BlockSpec((None, 1, MEM_TOKENS, MEM_WIDTH), lambda i: (layer, i // per_seq, 0, 0)),
            _layer_spec(w_rest, layer, pipeline_mode=single),
            _layer_spec(cw, layer),
        ],
        out_specs=pl.BlockSpec((rows, MIX_WIDTH), lambda i: (i, 0)),
        out_shape=jax.ShapeDtypeStruct((tokens, MIX_WIDTH), BF16),
        compiler_params=pltpu.CompilerParams(dimension_semantics=("parallel",), vmem_limit_bytes=VMEM_LIMIT),
        name="mix",
    )(h2, h2, h2, a2, mk, mv, w_rest, cw)


def _out_proj_kernel(y_ref, w_o_ref, x_ref, g_ref, o_ref):
    o = _dot(y_ref[...], w_o_ref[...])
    o_ref[...] = x_ref[...] + _rms(o, g_ref[...])


def _out_proj(y2, w_o, x2, g, layer):
    rows = OUT_ROWS
    tokens = x2.shape[0]
    return pl.pallas_call(
        _out_proj_kernel,
        grid=(tokens // rows,),
        in_specs=[
            pl.BlockSpec((rows, MIX_WIDTH), lambda i: (i, 0)),
            _layer_spec(w_o, layer, pipeline_mode=pl.Buffered(1)),
            pl.BlockSpec((rows, D_MODEL), lambda i: (i, 0)),
            _layer_spec(g, layer),
        ],
        out_specs=pl.BlockSpec((rows, D_MODEL), lambda i: (i, 0)),
        out_shape=jax.ShapeDtypeStruct(x2.shape, F32),
        compiler_params=pltpu.CompilerParams(dimension_semantics=("parallel",), vmem_limit_bytes=VMEM_LIMIT),
        name="out_proj",
    )(y2, w_o, x2, g)


def _rope_inv_freq_col():
    inv_freq = 1.0 / (ROPE_THETA ** (jnp.arange(0, QK_ROPE_DIM, 2, dtype=F32) / QK_ROPE_DIM))
    return inv_freq[:, None]


def kernel(x, mem, positions, pre_norm_g, w_in, q_norm_g, w_uq, kv_norm_g, w_ukv, conv_w, mem_norm_g, w_mk, w_mv,
           w_o, post_norm_g):
    batch, seq, _ = x.shape
    tokens = batch * seq
    assert seq % PREP_ROWS == 0 and seq % MIX_ROWS == 0 and seq % ATTN_Q == 0 and seq % ATTN_K == 0
    assert tokens % OUT_ROWS == 0 and tokens % ROPE_COLS == 0 and MLA_HEADS % ATTN_HEADS == 0

    c1 = Q_LORA_RANK + KV_LORA_RANK + QK_ROPE_DIM
    w_in_t = jnp.swapaxes(w_in, 1, 2)
    w_lat = jnp.pad(w_in_t[:, :c1, :], ((0, 0), (0, LANES - QK_ROPE_DIM), (0, 0))).astype(BF16)
    w_rest = w_in_t[:, c1:, :].astype(BF16)

    w_uq_h = w_uq.reshape(DEPTH, Q_LORA_RANK, MLA_HEADS, QK_HEAD_DIM)
    w_uq_nope = w_uq_h[..., :QK_NOPE_DIM].reshape(DEPTH, Q_LORA_RANK, MLA_WIDTH)
    w_uq_rope = w_uq_h[..., QK_NOPE_DIM:].reshape(DEPTH, Q_LORA_RANK, MLA_HEADS * QK_ROPE_DIM)
    w_uq_t = jnp.swapaxes(jnp.concatenate([w_uq_nope, w_uq_rope], axis=-1), 1, 2).astype(BF16)

    w_ukv_h = w_ukv.reshape(DEPTH, KV_LORA_RANK, MLA_HEADS, QK_NOPE_DIM + V_HEAD_DIM)
    w_k = w_ukv_h[..., :QK_NOPE_DIM].reshape(DEPTH, KV_LORA_RANK, MLA_WIDTH).astype(BF16)
    w_vt = jnp.swapaxes(w_ukv_h[..., QK_NOPE_DIM:].reshape(DEPTH, KV_LORA_RANK, MLA_WIDTH), 1, 2).astype(BF16)

    w_o_b = w_o.astype(BF16)
    gpre, gq, gkv, gpost = (g[:, None, :] for g in (pre_norm_g, q_norm_g, kv_norm_g, post_norm_g))
    rope = _rope_tables(positions.reshape(1, tokens), _rope_inv_freq_col())
    mk, mv = _mem_kv(mem, mem_norm_g[:, None, :], w_mk.astype(BF16), w_mv.astype(BF16))

    x2 = x.reshape(tokens, D_MODEL)
    for l in range(DEPTH):
        q_t, k, vt, h2 = _mla_prep(x2, rope, l, gpre, w_lat, gq, w_uq_t, gkv, w_k, w_vt, batch, seq)
        a = _mla_attn(q_t, k, vt)
        y = _mix(h2, a.reshape(tokens, MLA_WIDTH), mk, mv, l, w_rest, conv_w, batch, seq)
        x2 = _out_proj(y, w_o_b, x2, gpost, l)
    return x2.reshape(batch, seq, D_MODEL)
```

```python
import functools
import math

import jax
import jax.numpy as jnp
from jax import lax
from jax.experimental import pallas as pl
from jax.experimental.pallas import tpu as pltpu

D_MODEL = 2048
DEPTH = 4
MEM_TOKENS = 256
EPS = 1e-6
ROPE_THETA = 10000.0
MLA_HEADS = 8
QK_NOPE_DIM = 128
QK_ROPE_DIM = 64
QK_HEAD_DIM = QK_NOPE_DIM + QK_ROPE_DIM
V_HEAD_DIM = 128
Q_LORA_RANK = 512
KV_LORA_RANK = 256
MLA_WIDTH = MLA_HEADS * V_HEAD_DIM
CONV_WIDTH = 512
MEM_HEADS = 4
MEM_HEAD_DIM = 128
MEM_WIDTH = MEM_HEADS * MEM_HEAD_DIM
MIX_WIDTH = MLA_WIDTH + CONV_WIDTH + MEM_WIDTH

LANES = 128
SUBLANES = 8
QK_PAD_DIM = QK_NOPE_DIM + LANES
ONES_ROWS = 2 * SUBLANES
HALO = 2 * SUBLANES
VT_ROWS = V_HEAD_DIM + ONES_ROWS

BF16 = jnp.bfloat16
F32 = jnp.float32

PREP_ROWS = 1024
ROPE_COLS = 2048
MIX_ROWS = 1024
GATE_COLS = 512
OUT_ROWS = 1024
ATTN_Q = 256
ATTN_K = 512
ATTN_HEADS = 2
ATTN_SPLIT = 2
VMEM_LIMIT = 60 * 1024 * 1024

_NT = (((1,), (1,)), ((), ()))


def _rms(x, g):
    return x * lax.rsqrt(jnp.mean(x * x, axis=-1, keepdims=True) + EPS) * g


def _dot(a, b):
    return jnp.dot(a, b, preferred_element_type=F32)


def _dot_nt(a, b):
    return lax.dot_general(a, b, _NT, preferred_element_type=F32)


def _layer_spec(stacked, layer, **kwargs):
    return pl.BlockSpec((None,) + stacked.shape[1:], lambda i: (layer, 0, 0), **kwargs)


def _rope_tables_kernel(pos_ref, invf_ref, cos_t_ref, sin_t_ref):
    ang = invf_ref[...] * pos_ref[...].astype(F32)
    cos_t_ref[...] = jnp.cos(ang)
    sin_t_ref[...] = jnp.sin(ang)


def _rope_tables(pos_row, invf_col):
    tokens = pos_row.shape[1]
    cols = ROPE_COLS
    half = QK_ROPE_DIM // 2
    table = jax.ShapeDtypeStruct((half, tokens), F32)
    return pl.pallas_call(
        _rope_tables_kernel,
        grid=(tokens // cols,),
        in_specs=[pl.BlockSpec((1, cols), lambda i: (0, i)), pl.BlockSpec(invf_col.shape, lambda i: (0, 0))],
        out_specs=[pl.BlockSpec((half, cols), lambda i: (0, i)), pl.BlockSpec((half, cols), lambda i: (0, i))],
        out_shape=[table, table],
        compiler_params=pltpu.CompilerParams(dimension_semantics=("parallel",)),
        name="rope_tables",
    )(pos_row, invf_col)


def _mla_prep_kernel(x_ref, cos_t_ref, sin_t_ref, gpre_ref, w_lat_ref, gq_ref, w_uqt_ref, gkv_ref,
                     w_k_ref, w_vt_ref, qt_ref, k_ref, vt_ref, h_ref):
    h = _rms(x_ref[...], gpre_ref[...]).astype(BF16)
    h_ref[...] = h
    z = _dot_nt(h, w_lat_ref[...])
    qn = _rms(z[:, :Q_LORA_RANK], gq_ref[...]).astype(BF16)
    kvn = _rms(z[:, Q_LORA_RANK:Q_LORA_RANK + KV_LORA_RANK], gkv_ref[...]).astype(BF16)
    kpe_t = z[:, Q_LORA_RANK + KV_LORA_RANK:].T

    half = QK_ROPE_DIM // 2
    cos_t = cos_t_ref[...]
    sin_t = sin_t_ref[...]

    def rope_t(t):
        x1, x2 = t[:half], t[half:]
        return x1 * cos_t - x2 * sin_t, x2 * cos_t + x1 * sin_t

    rows = z.shape[0]
    k1, k2 = rope_t(kpe_t[:QK_ROPE_DIM])
    kpe_r = jnp.concatenate([k1, k2, jnp.zeros((LANES - QK_ROPE_DIM, rows), F32)], axis=0).T.astype(BF16)

    q_scale = (QK_HEAD_DIM ** -0.5) * math.log2(math.e)
    q_t = _dot_nt(w_uqt_ref[...], qn) * q_scale
    kn = _dot(kvn, w_k_ref[...])
    vt = _dot_nt(w_vt_ref[...], kvn)
    for hd in range(MLA_HEADS):
        lo, hi = hd * LANES, (hd + 1) * LANES
        r_lo = MLA_WIDTH + hd * QK_ROPE_DIM
        q1, q2 = rope_t(q_t[r_lo:r_lo + QK_ROPE_DIM])
        qt_ref[0, hd, :QK_NOPE_DIM, :] = q_t[lo:hi].astype(BF16)
        qt_ref[0, hd, QK_NOPE_DIM:QK_NOPE_DIM + half, :] = q1.astype(BF16)
        qt_ref[0, hd, QK_NOPE_DIM + half:QK_HEAD_DIM, :] = q2.astype(BF16)
        qt_ref[0, hd, QK_HEAD_DIM:, :] = jnp.zeros((QK_PAD_DIM - QK_HEAD_DIM, rows), BF16)
        k_ref[0, hd, :, :QK_NOPE_DIM] = kn[:, lo:hi].astype(BF16)
        k_ref[0, hd, :, QK_NOPE_DIM:] = kpe_r
        vt_ref[0, hd, :V_HEAD_DIM, :] = vt[lo:hi, :].astype(BF16)
        vt_ref[0, hd, V_HEAD_DIM:, :] = jnp.ones((ONES_ROWS, rows), BF16)


def _mla_prep(x2, rope, layer, gpre, w_lat, gq, w_uq_t, gkv, w_k, w_vt, batch, seq):
    rows = PREP_ROWS
    per_seq = seq // rows
    half = QK_ROPE_DIM // 2
    head_rows = lambda i: (i // per_seq, 0, i % per_seq, 0)
    head_cols = lambda i: (i // per_seq, 0, 0, i % per_seq)
    return pl.pallas_call(
        _mla_prep_kernel,
        grid=(batch * seq // rows,),
        in_specs=[
            pl.BlockSpec((rows, D_MODEL), lambda i: (i, 0)),
            pl.BlockSpec((half, rows), lambda i: (0, i)),
            pl.BlockSpec((half, rows), lambda i: (0, i)),
            _layer_spec(gpre, layer),
            _layer_spec(w_lat, layer),
            _layer_spec(gq, layer),
            _layer_spec(w_uq_t, layer),
            _layer_spec(gkv, layer),
            _layer_spec(w_k, layer),
            _layer_spec(w_vt, layer),
        ],
        out_specs=[
            pl.BlockSpec((1, MLA_HEADS, QK_PAD_DIM, rows), head_cols),
            pl.BlockSpec((1, MLA_HEADS, rows, QK_PAD_DIM), head_rows),
            pl.BlockSpec((1, MLA_HEADS, VT_ROWS, rows), head_cols),
            pl.BlockSpec((rows, D_MODEL), lambda i: (i, 0)),
        ],
        out_shape=[
            jax.ShapeDtypeStruct((batch, MLA_HEADS, QK_PAD_DIM, seq), BF16),
            jax.ShapeDtypeStruct((batch, MLA_HEADS, seq, QK_PAD_DIM), BF16),
            jax.ShapeDtypeStruct((batch, MLA_HEADS, VT_ROWS, seq), BF16),
            jax.ShapeDtypeStruct((batch * seq, D_MODEL), BF16),
        ],
        compiler_params=pltpu.CompilerParams(dimension_semantics=("parallel",), vmem_limit_bytes=VMEM_LIMIT),
        name="mla_prep",
    )(x2, *rope, gpre, w_lat, gq, w_uq_t, gkv, w_k, w_vt)


def _mla_attn_kernel(qt_ref, k_ref, vt_ref, o_ref, *s_bufs, seq):
    n_blocks = seq // ATTN_Q // ATTN_SPLIT
    n_chunks = seq // ATTN_K
    streams = [(s // ATTN_SPLIT, (s % ATTN_SPLIT) * n_blocks, buf) for s, buf in enumerate(s_bufs)]

    def q_block(st, j):
        hd, base, _ = st
        return qt_ref[0, hd, :, pl.ds(pl.multiple_of((base + j) * ATTN_Q, ATTN_Q), ATTN_Q)]

    def read_probs(st, c, m):
        return jnp.exp2((st[2][c * ATTN_K:(c + 1) * ATTN_K, :] - m).astype(BF16))

    def write_scores(st, c, q_t):
        s = _dot(k_ref[0, st[0], c * ATTN_K:(c + 1) * ATTN_K, :], q_t)
        st[2][c * ATTN_K:(c + 1) * ATTN_K, :] = s
        return jnp.max(s, axis=0, keepdims=True)

    def weighted_values(st, c, m, acc):
        part = _dot(vt_ref[0, st[0], :, c * ATTN_K:(c + 1) * ATTN_K], read_probs(st, c, m))
        return part if acc is None else acc + part

    def emit(st, j, acc):
        hd, base, _ = st
        o_t = acc[:V_HEAD_DIM] * (1.0 / acc[V_HEAD_DIM:V_HEAD_DIM + 1])
        rows = pl.ds(pl.multiple_of((base + j) * ATTN_Q, ATTN_Q), ATTN_Q)
        o_ref[0, rows, hd * V_HEAD_DIM:(hd + 1) * V_HEAD_DIM] = o_t.T.astype(o_ref.dtype)

    def col_max(parts):
        return functools.reduce(jnp.maximum, parts)

    def first_scores(st):
        q_t = q_block(st, 0)
        return col_max([write_scores(st, c, q_t) for c in range(n_chunks)])

    def step(st, j, carry):
        m_prev, acc_prev = carry
        emit(st, jnp.maximum(j - 2, 0), acc_prev)
        q_t = q_block(st, j)
        acc, maxes = None, []
        for c in range(n_chunks):
            acc = weighted_values(st, c, m_prev, acc)
            maxes.append(write_scores(st, c, q_t))
        return col_max(maxes), acc

    def all_streams(j, carries):
        return tuple(step(st, j, carry) for st, carry in zip(streams, carries))

    init = tuple((first_scores(st), jnp.ones((VT_ROWS, ATTN_Q), F32)) for st in streams)
    carries = lax.fori_loop(1, n_blocks, all_streams, init)
    for st, (m, acc) in zip(streams, carries):
        emit(st, n_blocks - 2, acc)
        acc = None
        for c in range(n_chunks):
            acc = weighted_values(st, c, m, acc)
        emit(st, n_blocks - 1, acc)


def _mla_attn(q_t, k, vt):
    batch, heads, seq, _ = k.shape
    group = ATTN_HEADS
    return pl.pallas_call(
        functools.partial(_mla_attn_kernel, seq=seq),
        grid=(batch, heads // group),
        in_specs=[
            pl.BlockSpec((1, group, QK_PAD_DIM, seq), lambda b, g: (b, g, 0, 0)),
            pl.BlockSpec((1, group, seq, QK_PAD_DIM), lambda b, g: (b, g, 0, 0)),
            pl.BlockSpec((1, group, VT_ROWS, seq), lambda b, g: (b, g, 0, 0)),
        ],
        out_specs=pl.BlockSpec((1, seq, group * V_HEAD_DIM), lambda b, g: (b, 0, g)),
        out_shape=jax.ShapeDtypeStruct((batch, seq, MLA_WIDTH), BF16),
        scratch_shapes=[pltpu.VMEM((seq, ATTN_Q), F32)] * (group * ATTN_SPLIT),
        compiler_params=pltpu.CompilerParams(
            dimension_semantics=("parallel", "parallel"), vmem_limit_bytes=VMEM_LIMIT),
        name="mla_attn",
    )(q_t, k, vt)


def _mem_kv_kernel(mem_ref, g_ref, w_mk_ref, w_mv_ref, mk_ref, mv_ref):
    mem_n = _rms(mem_ref[0], g_ref[0]).astype(BF16)
    mk_ref[0, 0] = _dot(mem_n, w_mk_ref[0]).astype(BF16)
    mv_ref[0, 0] = _dot(mem_n, w_mv_ref[0]).astype(BF16)


def _mem_kv(mem, g, w_mk, w_mv):
    batch = mem.shape[0]
    out = jax.ShapeDtypeStruct((DEPTH, batch, MEM_TOKENS, MEM_WIDTH), BF16)
    return pl.pallas_call(
        _mem_kv_kernel,
        grid=(DEPTH, batch),
        in_specs=[
            pl.BlockSpec((1, MEM_TOKENS, D_MODEL), lambda l, b: (b, 0, 0)),
            pl.BlockSpec((1, 1, D_MODEL), lambda l, b: (l, 0, 0)),
            pl.BlockSpec((1, D_MODEL, MEM_WIDTH), lambda l, b: (l, 0, 0)),
            pl.BlockSpec((1, D_MODEL, MEM_WIDTH), lambda l, b: (l, 0, 0)),
        ],
        out_specs=[
            pl.BlockSpec((1, 1, MEM_TOKENS, MEM_WIDTH), lambda l, b: (l, b, 0, 0)),
            pl.BlockSpec((1, 1, MEM_TOKENS, MEM_WIDTH), lambda l, b: (l, b, 0, 0)),
        ],
        out_shape=[out, out],
        compiler_params=pltpu.CompilerParams(dimension_semantics=("parallel", "parallel")),
        name="mem_kv",
    )(mem, g, w_mk, w_mv)


def _mix_kernel(h_ref, hp_ref, hn_ref, a_ref, mk_ref, mv_ref, w_ref, cw_ref, y_ref, *, seq):
    c_gc, c_qm = CONV_WIDTH, 3 * CONV_WIDTH
    rows = h_ref.shape[0]
    i = pl.program_id(0)
    keep_prev = ((i * rows) % seq != 0).astype(BF16)
    keep_next = (((i + 1) * rows) % seq != 0).astype(BF16)
    h_ext = jnp.concatenate([hp_ref[...] * keep_prev, h_ref[...], hn_ref[...] * keep_next], axis=0)

    zc = _dot_nt(h_ext, w_ref[c_gc:c_qm, :])
    u = zc[:, :CONV_WIDTH] * zc[:, CONV_WIDTH:]
    ext = rows + 2 * HALO
    body = slice(HALO, HALO + rows)
    conv = (pltpu.roll(u, 1, 0)[body] * cw_ref[0:1, :] + u[body] * cw_ref[1:2, :]
            + pltpu.roll(u, ext - 1, 0)[body] * cw_ref[2:3, :])

    h = h_ref[...]
    c_gate = c_qm + MEM_WIDTH

    def gated(val, lo, hi):
        gate = _dot_nt(h, w_ref[c_gate + lo:c_gate + hi, :])
        y_ref[:, lo:hi] = (val * (gate * (1.0 / (1.0 + jnp.exp(-gate))))).astype(BF16)

    gated(_dot_nt(h, w_ref[:c_gc, :]) * conv, MLA_WIDTH, MLA_WIDTH + CONV_WIDTH)

    qm = _dot_nt(h, w_ref[c_qm:c_gate, :]) * (MEM_HEAD_DIM ** -0.5)
    m_out = []
    for hd in range(MEM_HEADS):
        lo, hi = hd * MEM_HEAD_DIM, (hd + 1) * MEM_HEAD_DIM
        s = _dot_nt(qm[:, lo:hi].astype(BF16), mk_ref[0, :, lo:hi])
        p = jnp.exp(s - jnp.max(s, axis=-1, keepdims=True))
        m_out.append(_dot(p.astype(BF16), mv_ref[0, :, lo:hi]) * (1.0 / jnp.sum(p, axis=-1, keepdims=True)))
    gated(jnp.concatenate(m_out, axis=1), MLA_WIDTH + CONV_WIDTH, MIX_WIDTH)

    for lo in range(0, MLA_WIDTH, GATE_COLS):
        gated(a_ref[:, lo:lo + GATE_COLS].astype(F32), lo, lo + GATE_COLS)


def _mix(h2, a2, mk, mv, layer, w_rest, cw, batch, seq):
    rows = MIX_ROWS
    tokens = batch * seq
    per_seq = seq // rows
    tiles = rows // HALO
    last_tile = tokens // HALO - 1
    single = pl.Buffered(1)
    return pl.pallas_call(
        functools.partial(_mix_kernel, seq=seq),
        grid=(tokens // rows,),
        in_specs=[
            pl.BlockSpec((rows, D_MODEL), lambda i: (i, 0)),
            pl.BlockSpec((HALO, D_MODEL), lambda i: (jnp.maximum(i * tiles - 1, 0), 0)),
            pl.BlockSpec((HALO, D_MODEL), lambda i: (jnp.minimum((i + 1) * tiles, last_tile), 0)),
            pl.BlockSpec((rows, MLA_WIDTH), lambda i: (i, 0)),
            pl.BlockSpec((None, 1, MEM_TOKENS, MEM_WIDTH), lambda i: (layer, i // per_seq, 0, 0)),
            pl.BlockSpec((None, 1, MEM_TOKENS, MEM_WIDTH), lambda i: (layer, i // per_seq, 0, 0)),
            _layer_spec(w_rest, layer, pipeline_mode=single),
            _layer_spec(cw, layer),
        ],
        out_specs=pl.BlockSpec((rows, MIX_WIDTH), lambda i: (i, 0)),
        out_shape=jax.ShapeDtypeStruct((tokens, MIX_WIDTH), BF16),
        compiler_params=pltpu.CompilerParams(dimension_semantics=("parallel",), vmem_limit_bytes=VMEM_LIMIT),
        name="mix",
    )(h2, h2, h2, a2, mk, mv, w_rest, cw)


def _out_proj_kernel(y_ref, w_o_ref, x_ref, g_ref, o_ref):
    o = _dot(y_ref[...], w_o_ref[...])
    o_ref[...] = x_ref[...] + _rms(o, g_ref[...])


def _out_proj(y2, w_o, x2, g, layer):
    rows = OUT_ROWS
    tokens = x2.shape[0]
    return pl.pallas_call(
        _out_proj_kernel,
        grid=(tokens // rows,),
        in_specs=[
            pl.BlockSpec((rows, MIX_WIDTH), lambda i: (i, 0)),
            _layer_spec(w_o, layer, pipeline_mode=pl.Buffered(1)),
            pl.BlockSpec((rows, D_MODEL), lambda i: (i, 0)),
            _layer_spec(g, layer),
        ],
        out_specs=pl.BlockSpec((rows, D_MODEL), lambda i: (i, 0)),
        out_shape=jax.ShapeDtypeStruct(x2.shape, F32),
        compiler_params=pltpu.CompilerParams(dimension_semantics=("parallel",), vmem_limit_bytes=VMEM_LIMIT),
        name="out_proj",
    )(y2, w_o, x2, g)


def _rope_inv_freq_col():
    inv_freq = 1.0 / (ROPE_THETA ** (jnp.arange(0, QK_ROPE_DIM, 2, dtype=F32) / QK_ROPE_DIM))
    return inv_freq[:, None]


def kernel(x, mem, positions, pre_norm_g, w_in, q_norm_g, w_uq, kv_norm_g, w_ukv, conv_w, mem_norm_g, w_mk, w_mv,
           w_o, post_norm_g):
    batch, seq, _ = x.shape
    tokens = batch * seq
    assert seq % PREP_ROWS == 0 and seq % MIX_ROWS == 0 and seq % (ATTN_Q * ATTN_SPLIT) == 0 and seq % ATTN_K == 0
    assert tokens % OUT_ROWS == 0 and tokens % ROPE_COLS == 0 and MLA_HEADS % ATTN_HEADS == 0

    c1 = Q_LORA_RANK + KV_LORA_RANK + QK_ROPE_DIM
    w_in_t = jnp.swapaxes(w_in, 1, 2)
    w_lat = jnp.pad(w_in_t[:, :c1, :], ((0, 0), (0, LANES - QK_ROPE_DIM), (0, 0))).astype(BF16)
    w_rest = w_in_t[:, c1:, :].astype(BF16)

    w_uq_h = w_uq.reshape(DEPTH, Q_LORA_RANK, MLA_HEADS, QK_HEAD_DIM)
    w_uq_nope = w_uq_h[..., :QK_NOPE_DIM].reshape(DEPTH, Q_LORA_RANK, MLA_WIDTH)
    w_uq_rope = w_uq_h[..., QK_NOPE_DIM:].reshape(DEPTH, Q_LORA_RANK, MLA_HEADS * QK_ROPE_DIM)
    w_uq_t = jnp.swapaxes(jnp.concatenate([w_uq_nope, w_uq_rope], axis=-1), 1, 2).astype(BF16)

    w_ukv_h = w_ukv.reshape(DEPTH, KV_LORA_RANK, MLA_HEADS, QK_NOPE_DIM + V_HEAD_DIM)
    w_k = w_ukv_h[..., :QK_NOPE_DIM].reshape(DEPTH, KV_LORA_RANK, MLA_WIDTH).astype(BF16)
    w_vt = jnp.swapaxes(w_ukv_h[..., QK_NOPE_DIM:].reshape(DEPTH, KV_LORA_RANK, MLA_WIDTH), 1, 2).astype(BF16)

    w_o_b = w_o.astype(BF16)
    gpre, gq, gkv, gpost = (g[:, None, :] for g in (pre_norm_g, q_norm_g, kv_norm_g, post_norm_g))
    rope = _rope_tables(positions.reshape(1, tokens), _rope_inv_freq_col())
    mk, mv = _mem_kv(mem, mem_norm_g[:, None, :], w_mk.astype(BF16), w_mv.astype(BF16))

    x2 = x.reshape(tokens, D_MODEL)
    for l in range(DEPTH):
        q_t, k, vt, h2 = _mla_prep(x2, rope, l, gpre, w_lat, gq, w_uq_t, gkv, w_k, w_vt, batch, seq)
        a = _mla_attn(q_t, k, vt)
        y = _mix(h2, a.reshape(tokens, MLA_WIDTH), mk, mv, l, w_rest, conv_w, batch, seq)
        x2 = _out_proj(y, w_o_b, x2, gpost, l)
    return x2.reshape(batch, seq, D_MODEL)
```

```python
import functools
import math

import jax
import jax.numpy as jnp
from jax import lax
from jax.experimental import pallas as pl
from jax.experimental.pallas import tpu as pltpu

D_MODEL = 2048
DEPTH = 4
MEM_TOKENS = 256
EPS = 1e-6
ROPE_THETA = 10000.0
MLA_HEADS = 8
QK_NOPE_DIM = 128
QK_ROPE_DIM = 64
QK_HEAD_DIM = QK_NOPE_DIM + QK_ROPE_DIM
V_HEAD_DIM = 128
Q_LORA_RANK = 512
KV_LORA_RANK = 256
MLA_WIDTH = MLA_HEADS * V_HEAD_DIM
CONV_WIDTH = 512
MEM_HEADS = 4
MEM_HEAD_DIM = 128
MEM_WIDTH = MEM_HEADS * MEM_HEAD_DIM
MIX_WIDTH = MLA_WIDTH + CONV_WIDTH + MEM_WIDTH

LANES = 128
SUBLANES = 8
QK_PAD_DIM = QK_NOPE_DIM + LANES
ONES_ROWS = 2 * SUBLANES
HALO = 2 * SUBLANES
VT_ROWS = V_HEAD_DIM + ONES_ROWS

BF16 = jnp.bfloat16
F32 = jnp.float32

PREP_ROWS = 1024
ROPE_COLS = 2048
MIX_ROWS = 1024
OUT_ROWS = 1024
ATTN_Q = 256
ATTN_K = 512
ATTN_HEADS = 2
ATTN_SPLIT = 2
VMEM_LIMIT = 60 * 1024 * 1024

_NT = (((1,), (1,)), ((), ()))


def _rms(x, g):
    return x * lax.rsqrt(jnp.mean(x * x, axis=-1, keepdims=True) + EPS) * g


def _dot(a, b):
    return jnp.dot(a, b, preferred_element_type=F32)


def _dot_nt(a, b):
    return lax.dot_general(a, b, _NT, preferred_element_type=F32)


def _layer_spec(stacked, layer, **kwargs):
    return pl.BlockSpec((None,) + stacked.shape[1:], lambda i: (layer, 0, 0), **kwargs)


def _rope_tables_kernel(pos_ref, invf_ref, cos_t_ref, sin_t_ref):
    ang = invf_ref[...] * pos_ref[...].astype(F32)
    cos_t_ref[...] = jnp.cos(ang)
    sin_t_ref[...] = jnp.sin(ang)


def _rope_tables(pos_row, invf_col):
    tokens = pos_row.shape[1]
    cols = ROPE_COLS
    half = QK_ROPE_DIM // 2
    table = jax.ShapeDtypeStruct((half, tokens), F32)
    return pl.pallas_call(
        _rope_tables_kernel,
        grid=(tokens // cols,),
        in_specs=[pl.BlockSpec((1, cols), lambda i: (0, i)), pl.BlockSpec(invf_col.shape, lambda i: (0, 0))],
        out_specs=[pl.BlockSpec((half, cols), lambda i: (0, i)), pl.BlockSpec((half, cols), lambda i: (0, i))],
        out_shape=[table, table],
        compiler_params=pltpu.CompilerParams(dimension_semantics=("parallel",)),
        name="rope_tables",
    )(pos_row, invf_col)


def _mla_prep_kernel(x_ref, cos_t_ref, sin_t_ref, gpre_ref, w_lat_ref, gq_ref, w_uqt_ref, gkv_ref,
                     w_k_ref, w_vt_ref, qt_ref, k_ref, vt_ref, h_ref):
    h = _rms(x_ref[...], gpre_ref[...]).astype(BF16)
    h_ref[...] = h
    z = _dot_nt(h, w_lat_ref[...])
    qn = _rms(z[:, :Q_LORA_RANK], gq_ref[...]).astype(BF16)
    kvn = _rms(z[:, Q_LORA_RANK:Q_LORA_RANK + KV_LORA_RANK], gkv_ref[...]).astype(BF16)
    kpe_t = z[:, Q_LORA_RANK + KV_LORA_RANK:].T

    half = QK_ROPE_DIM // 2
    cos_t = cos_t_ref[...]
    sin_t = sin_t_ref[...]

    def rope_t(t):
        x1, x2 = t[:half], t[half:]
        return x1 * cos_t - x2 * sin_t, x2 * cos_t + x1 * sin_t

    rows = z.shape[0]
    k1, k2 = rope_t(kpe_t[:QK_ROPE_DIM])
    kpe_r = jnp.concatenate([k1, k2, jnp.zeros((LANES - QK_ROPE_DIM, rows), F32)], axis=0).T.astype(BF16)

    q_scale = (QK_HEAD_DIM ** -0.5) * math.log2(math.e)
    q_t = _dot_nt(w_uqt_ref[...], qn) * q_scale
    kn = _dot(kvn, w_k_ref[...])
    vt = _dot_nt(w_vt_ref[...], kvn)
    for hd in range(MLA_HEADS):
        lo, hi = hd * LANES, (hd + 1) * LANES
        r_lo = MLA_WIDTH + hd * QK_ROPE_DIM
        q1, q2 = rope_t(q_t[r_lo:r_lo + QK_ROPE_DIM])
        qt_ref[0, hd, :QK_NOPE_DIM, :] = q_t[lo:hi].astype(BF16)
        qt_ref[0, hd, QK_NOPE_DIM:QK_NOPE_DIM + half, :] = q1.astype(BF16)
        qt_ref[0, hd, QK_NOPE_DIM + half:QK_HEAD_DIM, :] = q2.astype(BF16)
        qt_ref[0, hd, QK_HEAD_DIM:, :] = jnp.zeros((QK_PAD_DIM - QK_HEAD_DIM, rows), BF16)
        k_ref[0, hd, :, :QK_NOPE_DIM] = kn[:, lo:hi].astype(BF16)
        k_ref[0, hd, :, QK_NOPE_DIM:] = kpe_r
        vt_ref[0, hd, :V_HEAD_DIM, :] = vt[lo:hi, :].astype(BF16)
        vt_ref[0, hd, V_HEAD_DIM:, :] = jnp.ones((ONES_ROWS, rows), BF16)


def _mla_prep(x2, rope, layer, gpre, w_lat, gq, w_uq_t, gkv, w_k, w_vt, batch, seq):
    rows = PREP_ROWS
    per_seq = seq // rows
    half = QK_ROPE_DIM // 2
    head_rows = lambda i: (i // per_seq, 0, i % per_seq, 0)
    head_cols = lambda i: (i // per_seq, 0, 0, i % per_seq)
    return pl.pallas_call(
        _mla_prep_kernel,
        grid=(batch * seq // rows,),
        in_specs=[
            pl.BlockSpec((rows, D_MODEL), lambda i: (i, 0)),
            pl.BlockSpec((half, rows), lambda i: (0, i)),
            pl.BlockSpec((half, rows), lambda i: (0, i)),
            _layer_spec(gpre, layer),
            _layer_spec(w_lat, layer),
            _layer_spec(gq, layer),
            _layer_spec(w_uq_t, layer),
            _layer_spec(gkv, layer),
            _layer_spec(w_k, layer),
            _layer_spec(w_vt, layer),
        ],
        out_specs=[
            pl.BlockSpec((1, MLA_HEADS, QK_PAD_DIM, rows), head_cols),
            pl.BlockSpec((1, MLA_HEADS, rows, QK_PAD_DIM), head_rows),
            pl.BlockSpec((1, MLA_HEADS, VT_ROWS, rows), head_cols),
            pl.BlockSpec((rows, D_MODEL), lambda i: (i, 0)),
        ],
        out_shape=[
            jax.ShapeDtypeStruct((batch, MLA_HEADS, QK_PAD_DIM, seq), BF16),
            jax.ShapeDtypeStruct((batch, MLA_HEADS, seq, QK_PAD_DIM), BF16),
            jax.ShapeDtypeStruct((batch, MLA_HEADS, VT_ROWS, seq), BF16),
            jax.ShapeDtypeStruct((batch * seq, D_MODEL), BF16),
        ],
        compiler_params=pltpu.CompilerParams(dimension_semantics=("parallel",), vmem_limit_bytes=VMEM_LIMIT),
        name="mla_prep",
    )(x2, *rope, gpre, w_lat, gq, w_uq_t, gkv, w_k, w_vt)


def _mla_attn_kernel(qt_ref, k_ref, vt_ref, kn_ref, *rest, seq):
    qn_refs, o_ref = rest[:ATTN_SPLIT], rest[ATTN_SPLIT]
    s_bufs, m_scr = rest[ATTN_SPLIT + 1:-1], rest[-1]
    _mla_attn_body(qt_ref, k_ref, vt_ref, kn_ref, qn_refs, o_ref, s_bufs, m_scr, seq)


def _mla_attn_body(qt_ref, k_ref, vt_ref, kn_ref, qn_refs, o_ref, s_bufs, m_scr, seq):
    n_blocks = seq // ATTN_Q // ATTN_SPLIT
    n_chunks = seq // ATTN_K
    streams = [(s // ATTN_SPLIT, (s % ATTN_SPLIT) * n_blocks, buf) for s, buf in enumerate(s_bufs)]

    def q_block(st, j):
        hd, base, _ = st
        return qt_ref[0, hd, :, pl.ds(pl.multiple_of((base + j) * ATTN_Q, ATTN_Q), ATTN_Q)]

    def read_probs(st, c, m):
        return jnp.exp2((st[2][c * ATTN_K:(c + 1) * ATTN_K, :] - m).astype(BF16))

    def write_scores(st, c, q_t):
        s = _dot(k_ref[0, st[0], c * ATTN_K:(c + 1) * ATTN_K, :], q_t)
        st[2][c * ATTN_K:(c + 1) * ATTN_K, :] = s
        return jnp.max(s, axis=0, keepdims=True)

    def weighted_values(st, c, m, acc):
        part = _dot(vt_ref[0, st[0], :, c * ATTN_K:(c + 1) * ATTN_K], read_probs(st, c, m))
        return part if acc is None else acc + part

    def emit(st, j, acc):
        hd, base, _ = st
        o_t = acc[:V_HEAD_DIM] * (1.0 / acc[V_HEAD_DIM:V_HEAD_DIM + 1])
        rows = pl.ds(pl.multiple_of((base + j) * ATTN_Q, ATTN_Q), ATTN_Q)
        o_ref[0, rows, hd * V_HEAD_DIM:(hd + 1) * V_HEAD_DIM] = o_t.T.astype(o_ref.dtype)

    def col_max(parts):
        return functools.reduce(jnp.maximum, parts)

    def first_scores(st):
        q_t = q_block(st, 0)
        return col_max([write_scores(st, c, q_t) for c in range(n_chunks)])

    def step(st, j, carry):
        m_prev, acc_prev = carry
        emit(st, jnp.maximum(j - 2, 0), acc_prev)
        q_t = q_block(st, j)
        acc, maxes = None, []
        for c in range(n_chunks):
            acc = weighted_values(st, c, m_prev, acc)
            maxes.append(write_scores(st, c, q_t))
        return col_max(maxes), acc

    def all_streams(j, carries):
        return tuple(step(st, j, carry) for st, carry in zip(streams, carries))

    @pl.when(pl.program_id(1) == 0)
    def _():
        for s, st in enumerate(streams):
            m_scr[s] = first_scores(st)

    init = tuple((m_scr[s], jnp.ones((VT_ROWS, ATTN_Q), F32)) for s in range(len(streams)))
    carries = lax.fori_loop(1, n_blocks, all_streams, init)
    for s, (st, (m, acc)) in enumerate(zip(streams, carries)):
        emit(st, n_blocks - 2, acc)
        hd = st[0]
        q_next = qn_refs[s % ATTN_SPLIT][0, hd]
        acc, maxes = None, []
        for c in range(n_chunks):
            acc = weighted_values(st, c, m, acc)
            sc = _dot(kn_ref[0, hd, c * ATTN_K:(c + 1) * ATTN_K, :], q_next)
            st[2][c * ATTN_K:(c + 1) * ATTN_K, :] = sc
            maxes.append(jnp.max(sc, axis=0, keepdims=True))
        emit(st, n_blocks - 1, acc)
        m_scr[s] = col_max(maxes)


def _mla_attn(q_t, k, vt):
    batch, heads, seq, _ = k.shape
    group = ATTN_HEADS
    n_groups = heads // group
    streams = group * ATTN_SPLIT
    blocks_per_range = seq // ATTN_Q // ATTN_SPLIT
    nxt = lambda g: jnp.minimum(g + 1, n_groups - 1)
    next_q = [pl.BlockSpec((1, group, QK_PAD_DIM, ATTN_Q), functools.partial(
        lambda b, g, first: (b, nxt(g), 0, first), first=r * blocks_per_range)) for r in range(ATTN_SPLIT)]
    return pl.pallas_call(
        functools.partial(_mla_attn_kernel, seq=seq),
        grid=(batch, n_groups),
        in_specs=[
            pl.BlockSpec((1, group, QK_PAD_DIM, seq), lambda b, g: (b, g, 0, 0)),
            pl.BlockSpec((1, group, seq, QK_PAD_DIM), lambda b, g: (b, g, 0, 0)),
            pl.BlockSpec((1, group, VT_ROWS, seq), lambda b, g: (b, g, 0, 0)),
            pl.BlockSpec((1, group, seq, QK_PAD_DIM), lambda b, g: (b, nxt(g), 0, 0)),
            *next_q,
        ],
        out_specs=pl.BlockSpec((1, seq, group * V_HEAD_DIM), lambda b, g: (b, 0, g)),
        out_shape=jax.ShapeDtypeStruct((batch, seq, MLA_WIDTH), BF16),
        scratch_shapes=[pltpu.VMEM((seq, ATTN_Q), F32)] * streams + [pltpu.VMEM((streams, 1, ATTN_Q), F32)],
        compiler_params=pltpu.CompilerParams(
            dimension_semantics=("arbitrary", "arbitrary"), vmem_limit_bytes=VMEM_LIMIT),
        name="mla_attn",
    )(q_t, k, vt, k, *([q_t] * ATTN_SPLIT))


def _mem_kv_kernel(mem_ref, g_ref, w_mk_ref, w_mv_ref, mk_ref, mv_ref):
    mem_n = _rms(mem_ref[0], g_ref[0]).astype(BF16)
    mk_ref[0, 0] = _dot(mem_n, w_mk_ref[0]).astype(BF16)
    mv_ref[0, 0] = _dot(mem_n, w_mv_ref[0]).astype(BF16)


def _mem_kv(mem, g, w_mk, w_mv):
    batch = mem.shape[0]
    out = jax.ShapeDtypeStruct((DEPTH, batch, MEM_TOKENS, MEM_WIDTH), BF16)
    return pl.pallas_call(
        _mem_kv_kernel,
        grid=(DEPTH, batch),
        in_specs=[
            pl.BlockSpec((1, MEM_TOKENS, D_MODEL), lambda l, b: (b, 0, 0)),
            pl.BlockSpec((1, 1, D_MODEL), lambda l, b: (l, 0, 0)),
            pl.BlockSpec((1, D_MODEL, MEM_WIDTH), lambda l, b: (l, 0, 0)),
            pl.BlockSpec((1, D_MODEL, MEM_WIDTH), lambda l, b: (l, 0, 0)),
        ],
        out_specs=[
            pl.BlockSpec((1, 1, MEM_TOKENS, MEM_WIDTH), lambda l, b: (l, b, 0, 0)),
            pl.BlockSpec((1, 1, MEM_TOKENS, MEM_WIDTH), lambda l, b: (l, b, 0, 0)),
        ],
        out_shape=[out, out],
        compiler_params=pltpu.CompilerParams(dimension_semantics=("parallel", "parallel")),
        name="mem_kv",
    )(mem, g, w_mk, w_mv)


def _mix_kernel(h_ref, hp_ref, hn_ref, a_ref, mk_ref, mv_ref, w_ref, cw_ref, y_ref, *, seq):
    c_gc, c_qm = CONV_WIDTH, 3 * CONV_WIDTH
    rows = h_ref.shape[0]
    i = pl.program_id(0)
    keep_prev = ((i * rows) % seq != 0).astype(BF16)
    keep_next = (((i + 1) * rows) % seq != 0).astype(BF16)
    h_ext = jnp.concatenate([hp_ref[...] * keep_prev, h_ref[...], hn_ref[...] * keep_next], axis=0)

    zc = _dot_nt(h_ext, w_ref[c_gc:c_qm, :])
    u = zc[:, :CONV_WIDTH] * zc[:, CONV_WIDTH:]
    ext = rows + 2 * HALO
    body = slice(HALO, HALO + rows)
    conv = (pltpu.roll(u, 1, 0)[body] * cw_ref[0:1, :] + u[body] * cw_ref[1:2, :]
            + pltpu.roll(u, ext - 1, 0)[body] * cw_ref[2:3, :])

    h = h_ref[...]
    c_gate = c_qm + MEM_WIDTH

    def gated(val, lo, hi):
        gate = _dot_nt(h, w_ref[c_gate + lo:c_gate + hi, :])
        y_ref[:, lo:hi] = (val * (gate * (1.0 / (1.0 + jnp.exp(-gate))))).astype(BF16)

    gated(_dot_nt(h, w_ref[:c_gc, :]) * conv, MLA_WIDTH, MLA_WIDTH + CONV_WIDTH)

    qm = _dot_nt(h, w_ref[c_qm:c_gate, :]) * (MEM_HEAD_DIM ** -0.5)
    m_out = []
    for hd in range(MEM_HEADS):
        lo, hi = hd * MEM_HEAD_DIM, (hd + 1) * MEM_HEAD_DIM
        s = _dot_nt(qm[:, lo:hi].astype(BF16), mk_ref[0, :, lo:hi])
        p = jnp.exp(s - jnp.max(s, axis=-1, keepdims=True))
        m_out.append(_dot(p.astype(BF16), mv_ref[0, :, lo:hi]) * (1.0 / jnp.sum(p, axis=-1, keepdims=True)))
    gated(jnp.concatenate(m_out, axis=1), MLA_WIDTH + CONV_WIDTH, MIX_WIDTH)

    gated(a_ref[...].astype(F32), 0, MLA_WIDTH)


def _mix(h2, a2, mk, mv, layer, w_rest, cw, batch, seq):
    rows = MIX_ROWS
    tokens = batch * seq
    per_seq = seq // rows
    tiles = rows // HALO
    last_tile = tokens // HALO - 1
    single = pl.Buffered(1)
    return pl.pallas_call(
        functools.partial(_mix_kernel, seq=seq),
        grid=(tokens // rows,),
        in_specs=[
            pl.BlockSpec((rows, D_MODEL), lambda i: (i, 0)),
            pl.BlockSpec((HALO, D_MODEL), lambda i: (jnp.maximum(i * tiles - 1, 0), 0)),
            pl.BlockSpec((HALO, D_MODEL), lambda i: (jnp.minimum((i + 1) * tiles, last_tile), 0)),
            pl.BlockSpec((rows, MLA_WIDTH), lambda i: (i, 0)),
            pl.BlockSpec((None, 1, MEM_TOKENS, MEM_WIDTH), lambda i: (layer, i // per_seq, 0, 0)),
            pl.BlockSpec((None, 1, MEM_TOKENS, MEM_WIDTH), lambda i: (layer, i // per_seq, 0, 0)),
            _layer_spec(w_rest, layer, pipeline_mode=single),
            _layer_spec(cw, layer),
        ],
        out_specs=pl.BlockSpec((rows, MIX_WIDTH), lambda i: (i, 0)),
        out_shape=jax.ShapeDtypeStruct((tokens, MIX_WIDTH), BF16),
        compiler_params=pltpu.CompilerParams(dimension_semantics=("parallel",), vmem_limit_bytes=VMEM_LIMIT),
        name="mix",
    )(h2, h2, h2, a2, mk, mv, w_rest, cw)


def _out_proj_kernel(y_ref, w_o_ref, x_ref, g_ref, o_ref):
    o = _dot(y_ref[...], w_o_ref[...])
    o_ref[...] = x_ref[...] + _rms(o, g_ref[...])


def _out_proj(y2, w_o, x2, g, layer):
    rows = OUT_ROWS
    tokens = x2.shape[0]
    return pl.pallas_call(
        _out_proj_kernel,
        grid=(tokens // rows,),
        in_specs=[
            pl.BlockSpec((rows, MIX_WIDTH), lambda i: (i, 0)),
            _layer_spec(w_o, layer, pipeline_mode=pl.Buffered(1)),
            pl.BlockSpec((rows, D_MODEL), lambda i: (i, 0)),
            _layer_spec(g, layer),
        ],
        out_specs=pl.BlockSpec((rows, D_MODEL), lambda i: (i, 0)),
        out_shape=jax.ShapeDtypeStruct(x2.shape, F32),
        compiler_params=pltpu.CompilerParams(dimension_semantics=("parallel",), vmem_limit_bytes=VMEM_LIMIT),
        name="out_proj",
    )(y2, w_o, x2, g)


def _rope_inv_freq_col():
    inv_freq = 1.0 / (ROPE_THETA ** (jnp.arange(0, QK_ROPE_DIM, 2, dtype=F32) / QK_ROPE_DIM))
    return inv_freq[:, None]


def kernel(x, mem, positions, pre_norm_g, w_in, q_norm_g, w_uq, kv_norm_g, w_ukv, conv_w, mem_norm_g, w_mk, w_mv,
           w_o, post_norm_g):
    batch, seq, _ = x.shape
    tokens = batch * seq
    assert seq % PREP_ROWS == 0 and seq % MIX_ROWS == 0 and seq % (ATTN_Q * ATTN_SPLIT) == 0 and seq % ATTN_K == 0
    assert tokens % OUT_ROWS == 0 and tokens % ROPE_COLS == 0 and MLA_HEADS % ATTN_HEADS == 0

    c1 = Q_LORA_RANK + KV_LORA_RANK + QK_ROPE_DIM
    w_in_t = jnp.swapaxes(w_in, 1, 2)
    w_lat = jnp.pad(w_in_t[:, :c1, :], ((0, 0), (0, LANES - QK_ROPE_DIM), (0, 0))).astype(BF16)
    w_rest = w_in_t[:, c1:, :].astype(BF16)

    w_uq_h = w_uq.reshape(DEPTH, Q_LORA_RANK, MLA_HEADS, QK_HEAD_DIM)
    w_uq_nope = w_uq_h[..., :QK_NOPE_DIM].reshape(DEPTH, Q_LORA_RANK, MLA_WIDTH)
    w_uq_rope = w_uq_h[..., QK_NOPE_DIM:].reshape(DEPTH, Q_LORA_RANK, MLA_HEADS * QK_ROPE_DIM)
    w_uq_t = jnp.swapaxes(jnp.concatenate([w_uq_nope, w_uq_rope], axis=-1), 1, 2).astype(BF16)

    w_ukv_h = w_ukv.reshape(DEPTH, KV_LORA_RANK, MLA_HEADS, QK_NOPE_DIM + V_HEAD_DIM)
    w_k = w_ukv_h[..., :QK_NOPE_DIM].reshape(DEPTH, KV_LORA_RANK, MLA_WIDTH).astype(BF16)
    w_vt = jnp.swapaxes(w_ukv_h[..., QK_NOPE_DIM:].reshape(DEPTH, KV_LORA_RANK, MLA_WIDTH), 1, 2).astype(BF16)

    w_o_b = w_o.astype(BF16)
    gpre, gq, gkv, gpost = (g[:, None, :] for g in (pre_norm_g, q_norm_g, kv_norm_g, post_norm_g))
    rope = _rope_tables(positions.reshape(1, tokens), _rope_inv_freq_col())
    mk, mv = _mem_kv(mem, mem_norm_g[:, None, :], w_mk.astype(BF16), w_mv.astype(BF16))

    x2 = x.reshape(tokens, D_MODEL)
    for l in range(DEPTH):
        q_t, k, vt, h2 = _mla_prep(x2, rope, l, gpre, w_lat, gq, w_uq_t, gkv, w_k, w_vt, batch, seq)
        a = _mla_attn(q_t, k, vt)
        y = _mix(h2, a.reshape(tokens, MLA_WIDTH), mk, mv, l, w_rest, conv_w, batch, seq)
        x2 = _out_proj(y, w_o_b, x2, gpost, l)
    return x2.reshape(batch, seq, D_MODEL)
```

```python
import functools
import math

import jax
import jax.numpy as jnp
from jax import lax
from jax.experimental import pallas as pl
from jax.experimental.pallas import tpu as pltpu

D_MODEL = 2048
DEPTH = 4
MEM_TOKENS = 256
EPS = 1e-6
ROPE_THETA = 10000.0
MLA_HEADS = 8
QK_NOPE_DIM = 128
QK_ROPE_DIM = 64
QK_HEAD_DIM = QK_NOPE_DIM + QK_ROPE_DIM
V_HEAD_DIM = 128
Q_LORA_RANK = 512
KV_LORA_RANK = 256
MLA_WIDTH = MLA_HEADS * V_HEAD_DIM
CONV_WIDTH = 512
MEM_HEADS = 4
MEM_HEAD_DIM = 128
MEM_WIDTH = MEM_HEADS * MEM_HEAD_DIM
MIX_WIDTH = MLA_WIDTH + CONV_WIDTH + MEM_WIDTH

LANES = 128
SUBLANES = 8
QK_PAD_DIM = QK_NOPE_DIM + LANES
ONES_ROWS = 2 * SUBLANES
HALO = 2 * SUBLANES
VT_ROWS = V_HEAD_DIM + ONES_ROWS

BF16 = jnp.bfloat16
F32 = jnp.float32

PREP_ROWS = 1024
ROPE_COLS = 2048
MIX_ROWS = 1024
OUT_ROWS = 1024
ATTN_Q = 256
ATTN_K = 512
ATTN_HEADS = 2
ATTN_SPLIT = 2
VMEM_LIMIT = 60 * 1024 * 1024

_NT = (((1,), (1,)), ((), ()))


def _rms(x, g):
    return x * lax.rsqrt(jnp.mean(x * x, axis=-1, keepdims=True) + EPS) * g


def _dot(a, b):
    return jnp.dot(a, b, preferred_element_type=F32)


def _dot_nt(a, b):
    return lax.dot_general(a, b, _NT, preferred_element_type=F32)


def _layer_spec(stacked, layer, **kwargs):
    return pl.BlockSpec((None,) + stacked.shape[1:], lambda i: (layer, 0, 0), **kwargs)


def _rope_tables_kernel(pos_ref, invf_ref, cos_t_ref, sin_t_ref):
    ang = invf_ref[...] * pos_ref[...].astype(F32)
    cos_t_ref[...] = jnp.cos(ang)
    sin_t_ref[...] = jnp.sin(ang)


def _rope_tables(pos_row, invf_col):
    tokens = pos_row.shape[1]
    cols = ROPE_COLS
    half = QK_ROPE_DIM // 2
    table = jax.ShapeDtypeStruct((half, tokens), F32)
    return pl.pallas_call(
        _rope_tables_kernel,
        grid=(tokens // cols,),
        in_specs=[pl.BlockSpec((1, cols), lambda i: (0, i)), pl.BlockSpec(invf_col.shape, lambda i: (0, 0))],
        out_specs=[pl.BlockSpec((half, cols), lambda i: (0, i)), pl.BlockSpec((half, cols), lambda i: (0, i))],
        out_shape=[table, table],
        compiler_params=pltpu.CompilerParams(dimension_semantics=("parallel",)),
        name="rope_tables",
    )(pos_row, invf_col)


def _mla_prep_kernel(x_ref, cos_t_ref, sin_t_ref, gpre_ref, w_lat_ref, gq_ref, w_uqt_ref, gkv_ref,
                     w_k_ref, w_vt_ref, qt_ref, k_ref, vt_ref, h_ref):
    h = _rms(x_ref[...], gpre_ref[...]).astype(BF16)
    h_ref[...] = h
    z = _dot_nt(h, w_lat_ref[...])
    qn = _rms(z[:, :Q_LORA_RANK], gq_ref[...]).astype(BF16)
    kvn = _rms(z[:, Q_LORA_RANK:Q_LORA_RANK + KV_LORA_RANK], gkv_ref[...]).astype(BF16)
    kpe_t = z[:, Q_LORA_RANK + KV_LORA_RANK:].T

    half = QK_ROPE_DIM // 2
    cos_t = cos_t_ref[...]
    sin_t = sin_t_ref[...]

    def rope_t(t):
        x1, x2 = t[:half], t[half:]
        return x1 * cos_t - x2 * sin_t, x2 * cos_t + x1 * sin_t

    rows = z.shape[0]
    k1, k2 = rope_t(kpe_t[:QK_ROPE_DIM])
    kpe_r = jnp.concatenate([k1, k2, jnp.zeros((LANES - QK_ROPE_DIM, rows), F32)], axis=0).T.astype(BF16)

    q_scale = (QK_HEAD_DIM ** -0.5) * math.log2(math.e)
    q_t = _dot_nt(w_uqt_ref[...], qn) * q_scale
    kn = _dot(kvn, w_k_ref[...])
    vt = _dot_nt(w_vt_ref[...], kvn)
    for hd in range(MLA_HEADS):
        lo, hi = hd * LANES, (hd + 1) * LANES
        r_lo = MLA_WIDTH + hd * QK_ROPE_DIM
        q1, q2 = rope_t(q_t[r_lo:r_lo + QK_ROPE_DIM])
        qt_ref[0, hd, :QK_NOPE_DIM, :] = q_t[lo:hi].astype(BF16)
        qt_ref[0, hd, QK_NOPE_DIM:QK_NOPE_DIM + half, :] = q1.astype(BF16)
        qt_ref[0, hd, QK_NOPE_DIM + half:QK_HEAD_DIM, :] = q2.astype(BF16)
        qt_ref[0, hd, QK_HEAD_DIM:, :] = jnp.zeros((QK_PAD_DIM - QK_HEAD_DIM, rows), BF16)
        k_ref[0, hd, :, :QK_NOPE_DIM] = kn[:, lo:hi].astype(BF16)
        k_ref[0, hd, :, QK_NOPE_DIM:] = kpe_r
        vt_ref[0, hd, :V_HEAD_DIM, :] = vt[lo:hi, :].astype(BF16)
        vt_ref[0, hd, V_HEAD_DIM:, :] = jnp.ones((ONES_ROWS, rows), BF16)


def _mla_prep(x2, rope, layer, gpre, w_lat, gq, w_uq_t, gkv, w_k, w_vt, batch, seq):
    rows = PREP_ROWS
    per_seq = seq // rows
    half = QK_ROPE_DIM // 2
    head_rows = lambda i: (i // per_seq, 0, i % per_seq, 0)
    head_cols = lambda i: (i // per_seq, 0, 0, i % per_seq)
    return pl.pallas_call(
        _mla_prep_kernel,
        grid=(batch * seq // rows,),
        in_specs=[
            pl.BlockSpec((rows, D_MODEL), lambda i: (i, 0)),
            pl.BlockSpec((half, rows), lambda i: (0, i)),
            pl.BlockSpec((half, rows), lambda i: (0, i)),
            _layer_spec(gpre, layer),
            _layer_spec(w_lat, layer),
            _layer_spec(gq, layer),
            _layer_spec(w_uq_t, layer),
            _layer_spec(gkv, layer),
            _layer_spec(w_k, layer),
            _layer_spec(w_vt, layer),
        ],
        out_specs=[
            pl.BlockSpec((1, MLA_HEADS, QK_PAD_DIM, rows), head_cols),
            pl.BlockSpec((1, MLA_HEADS, rows, QK_PAD_DIM), head_rows),
            pl.BlockSpec((1, MLA_HEADS, VT_ROWS, rows), head_cols),
            pl.BlockSpec((rows, D_MODEL), lambda i: (i, 0)),
        ],
        out_shape=[
            jax.ShapeDtypeStruct((batch, MLA_HEADS, QK_PAD_DIM, seq), BF16),
            jax.ShapeDtypeStruct((batch, MLA_HEADS, seq, QK_PAD_DIM), BF16),
            jax.ShapeDtypeStruct((batch, MLA_HEADS, VT_ROWS, seq), BF16),
            jax.ShapeDtypeStruct((batch * seq, D_MODEL), BF16),
        ],
        compiler_params=pltpu.CompilerParams(dimension_semantics=("parallel",), vmem_limit_bytes=VMEM_LIMIT),
        name="mla_prep",
    )(x2, *rope, gpre, w_lat, gq, w_uq_t, gkv, w_k, w_vt)


def _mla_attn_kernel(qt_ref, k_ref, vt_ref, kn_ref, *rest, seq):
    qn_refs, o_ref = rest[:ATTN_SPLIT], rest[ATTN_SPLIT]
    s_bufs, m_scr = rest[ATTN_SPLIT + 1:-1], rest[-1]
    _mla_attn_body(qt_ref, k_ref, vt_ref, kn_ref, qn_refs, o_ref, s_bufs, m_scr, seq)


def _mla_attn_body(qt_ref, k_ref, vt_ref, kn_ref, qn_refs, o_ref, s_bufs, m_scr, seq):
    n_blocks = seq // ATTN_Q // ATTN_SPLIT
    n_chunks = seq // ATTN_K
    streams = [(s // ATTN_SPLIT, (s % ATTN_SPLIT) * n_blocks, buf) for s, buf in enumerate(s_bufs)]

    def q_block(st, j):
        hd, base, _ = st
        return qt_ref[0, hd, :, pl.ds(pl.multiple_of((base + j) * ATTN_Q, ATTN_Q), ATTN_Q)]

    def read_probs(st, c, m):
        return jnp.exp2((st[2][c * ATTN_K:(c + 1) * ATTN_K, :] - m).astype(BF16))

    def write_scores(st, c, q_t):
        s = _dot(k_ref[0, st[0], c * ATTN_K:(c + 1) * ATTN_K, :], q_t)
        st[2][c * ATTN_K:(c + 1) * ATTN_K, :] = s
        return jnp.max(s, axis=0, keepdims=True)

    def weighted_values(st, c, m, acc):
        part = _dot(vt_ref[0, st[0], :, c * ATTN_K:(c + 1) * ATTN_K], read_probs(st, c, m))
        return part if acc is None else acc + part

    def emit(st, j, acc):
        hd, base, _ = st
        o_t = acc[:V_HEAD_DIM] * (1.0 / acc[V_HEAD_DIM:V_HEAD_DIM + 1])
        rows = pl.ds(pl.multiple_of((base + j) * ATTN_Q, ATTN_Q), ATTN_Q)
        o_ref[0, rows, hd * V_HEAD_DIM:(hd + 1) * V_HEAD_DIM] = o_t.T.astype(o_ref.dtype)

    def col_max(parts):
        return functools.reduce(jnp.maximum, parts)

    def first_scores(st):
        q_t = q_block(st, 0)
        return col_max([write_scores(st, c, q_t) for c in range(n_chunks)])

    def step(st, j, carry):
        m_prev, acc_prev = carry
        emit(st, jnp.maximum(j - 2, 0), acc_prev)
        q_t = q_block(st, j)
        acc, maxes = None, []
        for c in range(n_chunks):
            acc = weighted_values(st, c, m_prev, acc)
            maxes.append(write_scores(st, c, q_t))
        return col_max(maxes), acc

    def all_streams(j, carries):
        return tuple(step(st, j, carry) for st, carry in zip(streams, carries))

    @pl.when((pl.program_id(0) == 0) & (pl.program_id(1) == 0))
    def _():
        for s, st in enumerate(streams):
            m_scr[s] = first_scores(st)

    init = tuple((m_scr[s], jnp.ones((VT_ROWS, ATTN_Q), F32)) for s in range(len(streams)))
    carries = lax.fori_loop(1, n_blocks, all_streams, init)
    for s, (st, (m, acc)) in enumerate(zip(streams, carries)):
        emit(st, n_blocks - 2, acc)
        hd = st[0]
        q_next = qn_refs[s % ATTN_SPLIT][0, hd]
        acc, maxes = None, []
        for c in range(n_chunks):
            acc = weighted_values(st, c, m, acc)
            sc = _dot(kn_ref[0, hd, c * ATTN_K:(c + 1) * ATTN_K, :], q_next)
            st[2][c * ATTN_K:(c + 1) * ATTN_K, :] = sc
            maxes.append(jnp.max(sc, axis=0, keepdims=True))
        emit(st, n_blocks - 1, acc)
        m_scr[s] = col_max(maxes)


def _mla_attn(q_t, k, vt):
    batch, heads, seq, _ = k.shape
    group = ATTN_HEADS
    n_groups = heads // group
    streams = group * ATTN_SPLIT
    blocks_per_range = seq // ATTN_Q // ATTN_SPLIT

    def nxt(b, g):
        flat = jnp.minimum(b * n_groups + g + 1, batch * n_groups - 1)
        return flat // n_groups, flat % n_groups

    next_q = [pl.BlockSpec((1, group, QK_PAD_DIM, ATTN_Q), functools.partial(
        lambda b, g, first: (*nxt(b, g), 0, first), first=r * blocks_per_range)) for r in range(ATTN_SPLIT)]
    return pl.pallas_call(
        functools.partial(_mla_attn_kernel, seq=seq),
        grid=(batch, n_groups),
        in_specs=[
            pl.BlockSpec((1, group, QK_PAD_DIM, seq), lambda b, g: (b, g, 0, 0)),
            pl.BlockSpec((1, group, seq, QK_PAD_DIM), lambda b, g: (b, g, 0, 0)),
            pl.BlockSpec((1, group, VT_ROWS, seq), lambda b, g: (b, g, 0, 0)),
            pl.BlockSpec((1, group, seq, QK_PAD_DIM), lambda b, g: (*nxt(b, g), 0, 0)),
            *next_q,
        ],
        out_specs=pl.BlockSpec((1, seq, group * V_HEAD_DIM), lambda b, g: (b, 0, g)),
        out_shape=jax.ShapeDtypeStruct((batch, seq, MLA_WIDTH), BF16),
        scratch_shapes=[pltpu.VMEM((seq, ATTN_Q), F32)] * streams + [pltpu.VMEM((streams, 1, ATTN_Q), F32)],
        compiler_params=pltpu.CompilerParams(
            dimension_semantics=("arbitrary", "arbitrary"), vmem_limit_bytes=VMEM_LIMIT),
        name="mla_attn",
    )(q_t, k, vt, k, *([q_t] * ATTN_SPLIT))


def _mem_kv_kernel(mem_ref, g_ref, w_mk_ref, w_mv_ref, mk_ref, mv_ref):
    mem_n = _rms(mem_ref[0], g_ref[0]).astype(BF16)
    mk_ref[0, 0] = _dot(mem_n, w_mk_ref[0]).astype(BF16)
    mv_ref[0, 0] = _dot(mem_n, w_mv_ref[0]).astype(BF16)


def _mem_kv(mem, g, w_mk, w_mv):
    batch = mem.shape[0]
    out = jax.ShapeDtypeStruct((DEPTH, batch, MEM_TOKENS, MEM_WIDTH), BF16)
    return pl.pallas_call(
        _mem_kv_kernel,
        grid=(DEPTH, batch),
        in_specs=[
            pl.BlockSpec((1, MEM_TOKENS, D_MODEL), lambda l, b: (b, 0, 0)),
            pl.BlockSpec((1, 1, D_MODEL), lambda l, b: (l, 0, 0)),
            pl.BlockSpec((1, D_MODEL, MEM_WIDTH), lambda l, b: (l, 0, 0)),
            pl.BlockSpec((1, D_MODEL, MEM_WIDTH), lambda l, b: (l, 0, 0)),
        ],
        out_specs=[
            pl.BlockSpec((1, 1, MEM_TOKENS, MEM_WIDTH), lambda l, b: (l, b, 0, 0)),
            pl.BlockSpec((1, 1, MEM_TOKENS, MEM_WIDTH), lambda l, b: (l, b, 0, 0)),
        ],
        out_shape=[out, out],
        compiler_params=pltpu.CompilerParams(dimension_semantics=("parallel", "parallel")),
        name="mem_kv",
    )(mem, g, w_mk, w_mv)


def _mix_kernel(h_ref, hp_ref, hn_ref, a_ref, mk_ref, mv_ref, w_ref, cw_ref, y_ref, *, seq):
    c_gc, c_qm = CONV_WIDTH, 3 * CONV_WIDTH
    rows = h_ref.shape[0]
    i = pl.program_id(0)
    keep_prev = ((i * rows) % seq != 0).astype(BF16)
    keep_next = (((i + 1) * rows) % seq != 0).astype(BF16)
    h_ext = jnp.concatenate([hp_ref[...] * keep_prev, h_ref[...], hn_ref[...] * keep_next], axis=0)

    zc = _dot_nt(h_ext, w_ref[c_gc:c_qm, :])
    u = zc[:, :CONV_WIDTH] * zc[:, CONV_WIDTH:]
    ext = rows + 2 * HALO
    body = slice(HALO, HALO + rows)
    conv = (pltpu.roll(u, 1, 0)[body] * cw_ref[0:1, :] + u[body] * cw_ref[1:2, :]
            + pltpu.roll(u, ext - 1, 0)[body] * cw_ref[2:3, :])

    h = h_ref[...]
    c_gate = c_qm + MEM_WIDTH

    def gated(val, lo, hi):
        gate = _dot_nt(h, w_ref[c_gate + lo:c_gate + hi, :])
        y_ref[:, lo:hi] = (val * (gate * (1.0 / (1.0 + jnp.exp(-gate))))).astype(BF16)

    gated(_dot_nt(h, w_ref[:c_gc, :]) * conv, MLA_WIDTH, MLA_WIDTH + CONV_WIDTH)

    qm = _dot_nt(h, w_ref[c_qm:c_gate, :]) * (MEM_HEAD_DIM ** -0.5)
    m_out = []
    for hd in range(MEM_HEADS):
        lo, hi = hd * MEM_HEAD_DIM, (hd + 1) * MEM_HEAD_DIM
        s = _dot_nt(qm[:, lo:hi].astype(BF16), mk_ref[0, :, lo:hi])
        p = jnp.exp(s - jnp.max(s, axis=-1, keepdims=True))
        m_out.append(_dot(p.astype(BF16), mv_ref[0, :, lo:hi]) * (1.0 / jnp.sum(p, axis=-1, keepdims=True)))
    gated(jnp.concatenate(m_out, axis=1), MLA_WIDTH + CONV_WIDTH, MIX_WIDTH)

    gated(a_ref[...].astype(F32), 0, MLA_WIDTH)


def _mix(h2, a2, mk, mv, layer, w_rest, cw, batch, seq):
    rows = MIX_ROWS
    tokens = batch * seq
    per_seq = seq // rows
    tiles = rows // HALO
    last_tile = tokens // HALO - 1
    single = pl.Buffered(1)
    return pl.pallas_call(
        functools.partial(_mix_kernel, seq=seq),
        grid=(tokens // rows,),
        in_specs=[
            pl.BlockSpec((rows, D_MODEL), lambda i: (i, 0)),
            pl.BlockSpec((HALO, D_MODEL), lambda i: (jnp.maximum(i * tiles - 1, 0), 0)),
            pl.BlockSpec((HALO, D_MODEL), lambda i: (jnp.minimum((i + 1) * tiles, last_tile), 0)),
            pl.BlockSpec((rows, MLA_WIDTH), lambda i: (i, 0)),
            pl.BlockSpec((None, 1, MEM_TOKENS, MEM_WIDTH), lambda i: (layer, i // per_seq, 0, 0)),
            pl.BlockSpec((None, 1, MEM_TOKENS, MEM_WIDTH), lambda i: (layer, i // per_seq, 0, 0)),
            _layer_spec(w_rest, layer, pipeline_mode=single),
            _layer_spec(cw, layer),
        ],
        out_specs=pl.BlockSpec((rows, MIX_WIDTH), lambda i: (i, 0)),
        out_shape=jax.ShapeDtypeStruct((tokens, MIX_WIDTH), BF16),
        compiler_params=pltpu.CompilerParams(dimension_semantics=("parallel",), vmem_limit_bytes=VMEM_LIMIT),
        name="mix",
    )(h2, h2, h2, a2, mk, mv, w_rest, cw)


def _out_proj_kernel(y_ref, w_o_ref, x_ref, g_ref, o_ref):
    o = _dot(y_ref[...], w_o_ref[...])
    o_ref[...] = x_ref[...] + _rms(o, g_ref[...])


def _out_proj(y2, w_o, x2, g, layer):
    rows = OUT_ROWS
    tokens = x2.shape[0]
    return pl.pallas_call(
        _out_proj_kernel,
        grid=(tokens // rows,),
        in_specs=[
            pl.BlockSpec((rows, MIX_WIDTH), lambda i: (i, 0)),
            _layer_spec(w_o, layer, pipeline_mode=pl.Buffered(1)),
            pl.BlockSpec((rows, D_MODEL), lambda i: (i, 0)),
            _layer_spec(g, layer),
        ],
        out_specs=pl.BlockSpec((rows, D_MODEL), lambda i: (i, 0)),
        out_shape=jax.ShapeDtypeStruct(x2.shape, F32),
        compiler_params=pltpu.CompilerParams(dimension_semantics=("parallel",), vmem_limit_bytes=VMEM_LIMIT),
        name="out_proj",
    )(y2, w_o, x2, g)


def _rope_inv_freq_col():
    inv_freq = 1.0 / (ROPE_THETA ** (jnp.arange(0, QK_ROPE_DIM, 2, dtype=F32) / QK_ROPE_DIM))
    return inv_freq[:, None]


def kernel(x, mem, positions, pre_norm_g, w_in, q_norm_g, w_uq, kv_norm_g, w_ukv, conv_w, mem_norm_g, w_mk, w_mv,
           w_o, post_norm_g):
    batch, seq, _ = x.shape
    tokens = batch * seq
    assert seq % PREP_ROWS == 0 and seq % MIX_ROWS == 0 and seq % (ATTN_Q * ATTN_SPLIT) == 0 and seq % ATTN_K == 0
    assert tokens % OUT_ROWS == 0 and tokens % ROPE_COLS == 0 and MLA_HEADS % ATTN_HEADS == 0

    c1 = Q_LORA_RANK + KV_LORA_RANK + QK_ROPE_DIM
    w_in_t = jnp.swapaxes(w_in, 1, 2)
    w_lat = jnp.pad(w_in_t[:, :c1, :], ((0, 0), (0, LANES - QK_ROPE_DIM), (0, 0))).astype(BF16)
    w_rest = w_in_t[:, c1:, :].astype(BF16)

    w_uq_h = w_uq.reshape(DEPTH, Q_LORA_RANK, MLA_HEADS, QK_HEAD_DIM)
    w_uq_nope = w_uq_h[..., :QK_NOPE_DIM].reshape(DEPTH, Q_LORA_RANK, MLA_WIDTH)
    w_uq_rope = w_uq_h[..., QK_NOPE_DIM:].reshape(DEPTH, Q_LORA_RANK, MLA_HEADS * QK_ROPE_DIM)
    w_uq_t = jnp.swapaxes(jnp.concatenate([w_uq_nope, w_uq_rope], axis=-1), 1, 2).astype(BF16)

    w_ukv_h = w_ukv.reshape(DEPTH, KV_LORA_RANK, MLA_HEADS, QK_NOPE_DIM + V_HEAD_DIM)
    w_k = w_ukv_h[..., :QK_NOPE_DIM].reshape(DEPTH, KV_LORA_RANK, MLA_WIDTH).astype(BF16)
    w_vt = jnp.swapaxes(w_ukv_h[..., QK_NOPE_DIM:].reshape(DEPTH, KV_LORA_RANK, MLA_WIDTH), 1, 2).astype(BF16)

    w_o_b = w_o.astype(BF16)
    gpre, gq, gkv, gpost = (g[:, None, :] for g in (pre_norm_g, q_norm_g, kv_norm_g, post_norm_g))
    rope = _rope_tables(positions.reshape(1, tokens), _rope_inv_freq_col())
    mk, mv = _mem_kv(mem, mem_norm_g[:, None, :], w_mk.astype(BF16), w_mv.astype(BF16))

    x2 = x.reshape(tokens, D_MODEL)
    for l in range(DEPTH):
        q_t, k, vt, h2 = _mla_prep(x2, rope, l, gpre, w_lat, gq, w_uq_t, gkv, w_k, w_vt, batch, seq)
        a = _mla_attn(q_t, k, vt)
        y = _mix(h2, a.reshape(tokens, MLA_WIDTH), mk, mv, l, w_rest, conv_w, batch, seq)
        x2 = _out_proj(y, w_o_b, x2, gpost, l)
    return x2.reshape(batch, seq, D_MODEL)
```
